```python
import math
import jax, jax.numpy as jnp
from jax import lax
import numpy as np

D_MODEL = 1024
BATCH = 1
SEQ = 16384
DEPTH = 2
DEC_BATCH = 32
DEC_SEQ = 1
PAST_LEN = 16384
PAGE_SIZE = 128

N_A_LAYERS = DEPTH // 2
N_B_LAYERS = DEPTH - N_A_LAYERS
GROUP_SIZE = 16
N_GROUPS = D_MODEL // GROUP_SIZE
STATE_DIM = 64
N_HEADS = 16
HEAD_DIM = D_MODEL // N_HEADS
D_FF = 4 * D_MODEL
Q_BLOCK = 128
EPS = 1e-6
DT_MIN = 1e-3
DT_MAX = 1e-1
FORGET_BIAS_INIT = 4.0

kernel_name = 'yoco_s5_fox_adaln_step'


def _rms(x, g):
    xf = x.astype(jnp.float32)
    y = xf * lax.rsqrt(jnp.mean(xf * xf, axis=-1, keepdims=True) + EPS)
    return (y * g.astype(jnp.float32)).astype(x.dtype)


def _ada_rms(x, g, shift, scale):
    return _rms(x, g) * (1 + scale[:, None, :]) + shift[:, None, :]


def _sq_relu_mlp(x, w1, w2):
    return jnp.square(jax.nn.relu(x @ w1)) @ w2


def _complex_affine_combine(e1, e2):
    a1r, a1i, b1r, b1i = e1
    a2r, a2i, b2r, b2i = e2
    return (a2r * a1r - a2i * a1i,
            a2r * a1i + a2i * a1r,
            a2r * b1r - a2i * b1i + b2r,
            a2r * b1i + a2i * b1r + b2i)


def _s5_mixer(u, h0_re, h0_im, lam_re, lam_im, log_dt, b_re, b_im, c_re, c_im, d_skip, w_glu):
    bsz, L, _ = u.shape
    f32 = jnp.float32
    dt = jnp.exp(log_dt.astype(f32))[:, None]
    lr = lam_re.astype(f32)
    li = lam_im.astype(f32)
    mag = jnp.exp(lr * dt)
    ang = li * dt
    ab_re = mag * jnp.cos(ang)
    ab_im = mag * jnp.sin(ang)
    den = lr * lr + li * li
    zr = ((ab_re - 1) * lr + ab_im * li) / den
    zi = (ab_im * lr - (ab_re - 1) * li) / den
    br = b_re.astype(f32)
    bi = b_im.astype(f32)
    bb_re = zr[..., None] * br - zi[..., None] * bi
    bb_im = zr[..., None] * bi + zi[..., None] * br
    uf = u.astype(f32)
    ug = uf.reshape(bsz, L, N_GROUPS, GROUP_SIZE)
    bu_re = jnp.einsum('blgh,gph->blgp', ug, bb_re)
    bu_im = jnp.einsum('blgh,gph->blgp', ug, bb_im)
    h0r = h0_re.astype(f32)
    h0i = h0_im.astype(f32)
    bu_re = bu_re.at[:, 0].add(ab_re * h0r - ab_im * h0i)
    bu_im = bu_im.at[:, 0].add(ab_re * h0i + ab_im * h0r)
    a_re = jnp.broadcast_to(ab_re, bu_re.shape)
    a_im = jnp.broadcast_to(ab_im, bu_im.shape)
    _, _, h_re, h_im = lax.associative_scan(_complex_affine_combine, (a_re, a_im, bu_re, bu_im), axis=1)
    y = (jnp.einsum('blgp,ghp->blgh', h_re, c_re.astype(f32))
         - jnp.einsum('blgp,ghp->blgh', h_im, c_im.astype(f32)))
    y = y.reshape(bsz, L, D_MODEL) + d_skip.astype(f32) * uf
    z = jax.nn.gelu(y).astype(u.dtype) @ w_glu
    val, gate = jnp.split(z, 2, axis=-1)
    out = val * jax.nn.sigmoid(gate)
    return out, h_re[:, -1].astype(h0_re.dtype), h_im[:, -1].astype(h0_im.dtype)


def _shared_kv(h, c, kv_ada_w, kv_ada_b, g_kv, w_kvf, b_f, k_norm):
    bsz, L, _ = h.shape
    sh, sc = jnp.split(c @ kv_ada_w + kv_ada_b, 2, axis=-1)
    xn = _ada_rms(h, g_kv, sh, sc)
    proj = xn @ w_kvf
    k = _rms(proj[..., :D_MODEL].reshape(bsz, L, N_HEADS, HEAD_DIM), k_norm)
    v = proj[..., D_MODEL:2 * D_MODEL].reshape(bsz, L, N_HEADS, HEAD_DIM)
    logf = jax.nn.log_sigmoid(proj[..., 2 * D_MODEL:].astype(jnp.float32) + b_f.astype(jnp.float32))
    return k, v, logf


def _fox_prompt(q, k, v, logf):
    bsz, L = q.shape[0], q.shape[1]
    n_blk = L // Q_BLOCK
    scale = 1.0 / math.sqrt(HEAD_DIM)
    F = jnp.cumsum(logf, axis=1)
    FT = F.transpose(0, 2, 1)
    qb = q.reshape(bsz, n_blk, Q_BLOCK, N_HEADS, HEAD_DIM).transpose(1, 0, 2, 3, 4)
    Fb = FT.reshape(bsz, N_HEADS, n_blk, Q_BLOCK).transpose(2, 0, 1, 3)
    kpos = jnp.arange(L)

    def block(args):
        i, qi, Fi = args
        s = jnp.einsum('bqhd,bkhd->bhqk', qi, k, preferred_element_type=jnp.float32) * scale
        bias = Fi[..., :, None] - FT[:, :, None, :]
        qpos = i * Q_BLOCK + jnp.arange(Q_BLOCK)
        mask = kpos[None, :] <= qpos[:, None]
        s = jnp.where(mask, s + bias, -jnp.inf)
        p = jax.nn.softmax(s, axis=-1)
        return jnp.einsum('bhqk,bkhd->bqhd', p.astype(v.dtype), v)

    o = lax.map(block, (jnp.arange(n_blk), qb, Fb))
    return o.transpose(1, 0, 2, 3, 4).reshape(bsz, L, N_HEADS, HEAD_DIM)


def _fox_sample(q, k_new, v_new, logf_new, cache_k, cache_v, cache_logf, page_table):
    T = q.shape[1]
    past = page_table.shape[1] * PAGE_SIZE
    scale = 1.0 / math.sqrt(HEAD_DIM)
    kpos = jnp.arange(past + T)
    qpos = past + jnp.arange(T)
    mask = kpos[None, :] <= qpos[:, None]

    def one(args):
        qi, kn, vn, fn, pages = args
        kp = cache_k[pages].reshape(past, N_HEADS, HEAD_DIM)
        vp = cache_v[pages].reshape(past, N_HEADS, HEAD_DIM)
        fp = cache_logf[pages].reshape(past, N_HEADS).astype(jnp.float32)
        kk = jnp.concatenate([kp, kn.astype(kp.dtype)], axis=0)
        vv = jnp.concatenate([vp, vn.astype(vp.dtype)], axis=0)
        F = jnp.cumsum(jnp.concatenate([fp, fn], axis=0), axis=0).T
        s = jnp.einsum('qhd,khd->hqk', qi, kk, preferred_element_type=jnp.float32) * scale
        bias = F[:, past:, None] - F[:, None, :]
        s = jnp.where(mask, s + bias, -jnp.inf)
        p = jax.nn.softmax(s, axis=-1)
        return jnp.einsum('hqk,khd->qhd', p.astype(vv.dtype), vv)

    return lax.map(one, (q, k_new, v_new, logf_new, page_table))


def _trunk(x, c, h0_re, h0_im, attend, p):
    bsz, L, _ = x.shape
    h = x
    ssm_re, ssm_im = [], []
    k = v = logf = None
    for layer in range(DEPTH):
        mod = c @ p['ada_w'][layer] + p['ada_b'][layer]
        sh_m, sc_m, g_m, sh_f, sc_f, g_f = jnp.split(mod, 6, axis=-1)
        xn = _ada_rms(h, p['g_mix'][layer], sh_m, sc_m)
        if layer < N_A_LAYERS:
            out, hr, hi = _s5_mixer(xn, h0_re[layer], h0_im[layer],
                                    p['ssm_lam_re'][layer], p['ssm_lam_im'][layer], p['ssm_log_dt'][layer],
                                    p['ssm_b_re'][layer], p['ssm_b_im'][layer],
                                    p['ssm_c_re'][layer], p['ssm_c_im'][layer],
                                    p['ssm_d'][layer], p['w_glu'][layer])
            ssm_re.append(hr)
            ssm_im.append(hi)
        else:
            j = layer - N_A_LAYERS
            if j == 0:
                k, v, logf = _shared_kv(h, c, p['kv_ada_w'], p['kv_ada_b'], p['g_kv'],
                                        p['w_kvf'], p['b_f'], p['k_norm'])
            q = _rms((xn @ p['w_q'][j]).reshape(bsz, L, N_HEADS, HEAD_DIM), p['q_norm'][j])
            o = attend(q, k, v, logf)
            out = o.reshape(bsz, L, D_MODEL) @ p['w_o'][j]
        h = h + g_m[:, None, :] * out
        xn = _ada_rms(h, p['g_mlp'][layer], sh_f, sc_f)
        h = h + g_f[:, None, :] * _sq_relu_mlp(xn, p['mlp_w1'][layer], p['mlp_w2'][layer])
    return h, jnp.stack(ssm_re), jnp.stack(ssm_im), k, v, logf


def setup_inputs(seed: int = 0) -> dict:
    key = jax.random.key(seed)
    ks = iter(jax.random.split(key, 48))
    f32 = jnp.float32

    def nrm(shape, std):
        return std * jax.random.normal(next(ks), shape, f32)

    n_pages = PAST_LEN // PAGE_SIZE
    n_phys = (5 * DEC_BATCH * n_pages) // 4
    page_table = jax.random.permutation(next(ks), n_phys)[:DEC_BATCH * n_pages]
    page_table = page_table.reshape(DEC_BATCH, n_pages).astype(jnp.int32)
    nidx = jnp.arange(STATE_DIM, dtype=f32)
    D = D_MODEL
    return {
        'x_prompt': nrm((BATCH, SEQ, D), 1.0),
        'x_sample': nrm((DEC_BATCH, DEC_SEQ, D), 1.0),
        'c_prompt': nrm((BATCH, D), 1.0),
        'c_sample': nrm((DEC_BATCH, D), 1.0),
        'state_ssm_re': nrm((N_A_LAYERS, DEC_BATCH, N_GROUPS, STATE_DIM), 0.1),
        'state_ssm_im': nrm((N_A_LAYERS, DEC_BATCH, N_GROUPS, STATE_DIM), 0.1),
        'cache_k': nrm((n_phys, PAGE_SIZE, N_HEADS, HEAD_DIM), 1.0),
        'cache_v': nrm((n_phys, PAGE_SIZE, N_HEADS, HEAD_DIM), 1.0),
        'cache_logf': jax.nn.log_sigmoid(FORGET_BIAS_INIT + nrm((n_phys, PAGE_SIZE, N_HEADS), 1.0)),
        'page_table': page_table,
        'ada_w': nrm((DEPTH, D, 6 * D), 0.5 / math.sqrt(D)),
        'ada_b': nrm((DEPTH, 6 * D), 0.02),
        'g_mix': 1.0 + nrm((DEPTH, D), 0.02),
        'g_mlp': 1.0 + nrm((DEPTH, D), 0.02),
        'mlp_w1': nrm((DEPTH, D, D_FF), 1.0 / math.sqrt(D)),
        'mlp_w2': nrm((DEPTH, D_FF, D), 1.0 / math.sqrt(D_FF)),
        'ssm_lam_re': -0.5 + nrm((N_A_LAYERS, N_GROUPS, STATE_DIM), 0.01),
        'ssm_lam_im': jnp.pi * nidx + nrm((N_A_LAYERS, N_GROUPS, STATE_DIM), 0.01),
        'ssm_log_dt': jax.random.uniform(next(ks), (N_A_LAYERS, N_GROUPS), f32,
                                         math.log(DT_MIN), math.log(DT_MAX)),
        'ssm_b_re': nrm((N_A_LAYERS, N_GROUPS, STATE_DIM, GROUP_SIZE), 1.0 / math.sqrt(2 * GROUP_SIZE)),
        'ssm_b_im': nrm((N_A_LAYERS, N_GROUPS, STATE_DIM, GROUP_SIZE), 1.0 / math.sqrt(2 * GROUP_SIZE)),
        'ssm_c_re': nrm((N_A_LAYERS, N_GROUPS, GROUP_SIZE, STATE_DIM), 1.0 / math.sqrt(2 * STATE_DIM)),
        'ssm_c_im': nrm((N_A_LAYERS, N_GROUPS, GROUP_SIZE, STATE_DIM), 1.0 / math.sqrt(2 * STATE_DIM)),
        'ssm_d': nrm((N_A_LAYERS, D), 1.0),
        'w_glu': nrm((N_A_LAYERS, D, 2 * D), 1.0 / math.sqrt(D)),
        'kv_ada_w': nrm((D, 2 * D), 0.5 / math.sqrt(D)),
        'kv_ada_b': nrm((2 * D,), 0.02),
        'g_kv': 1.0 + nrm((D,), 0.02),
        'w_kvf': nrm((D, 2 * D + N_HEADS), 1.0 / math.sqrt(D)),
        'b_f': FORGET_BIAS_INIT + nrm((N_HEADS,), 0.5),
        'k_norm': 1.0 + nrm((HEAD_DIM,), 0.02),
        'w_q': nrm((N_B_LAYERS, D, D), 1.0 / math.sqrt(D)),
        'q_norm': 1.0 + nrm((N_B_LAYERS, HEAD_DIM), 0.02),
        'w_o': nrm((N_B_LAYERS, D, D), 1.0 / math.sqrt(D)),
    }


def reference(x_prompt, x_sample, c_prompt, c_sample, state_ssm_re, state_ssm_im,
              cache_k, cache_v, cache_logf, page_table,
              ada_w, ada_b, g_mix, g_mlp, mlp_w1, mlp_w2,
              ssm_lam_re, ssm_lam_im, ssm_log_dt, ssm_b_re, ssm_b_im, ssm_c_re, ssm_c_im, ssm_d, w_glu,
              kv_ada_w, kv_ada_b, g_kv, w_kvf, b_f, k_norm, w_q, q_norm, w_o):
    params = dict(ada_w=ada_w, ada_b=ada_b, g_mix=g_mix, g_mlp=g_mlp, mlp_w1=mlp_w1, mlp_w2=mlp_w2,
                  ssm_lam_re=ssm_lam_re, ssm_lam_im=ssm_lam_im, ssm_log_dt=ssm_log_dt,
                  ssm_b_re=ssm_b_re, ssm_b_im=ssm_b_im, ssm_c_re=ssm_c_re, ssm_c_im=ssm_c_im,
                  ssm_d=ssm_d, w_glu=w_glu, kv_ada_w=kv_ada_w, kv_ada_b=kv_ada_b, g_kv=g_kv,
                  w_kvf=w_kvf, b_f=b_f, k_norm=k_norm, w_q=w_q, q_norm=q_norm, w_o=w_o)

    h0 = jnp.zeros((N_A_LAYERS, x_prompt.shape[0], N_GROUPS, STATE_DIM), state_ssm_re.dtype)
    y_prompt, sre_p, sim_p, k_p, v_p, f_p = _trunk(x_prompt, c_prompt, h0, h0, _fox_prompt, params)

    def attend_sample(q, k, v, f):
        return _fox_sample(q, k, v, f, cache_k, cache_v, cache_logf, page_table)

    y_sample, sre_s, sim_s, k_s, v_s, f_s = _trunk(x_sample, c_sample, state_ssm_re, state_ssm_im,
                                                    attend_sample, params)
    return (y_prompt, y_sample, sre_p, sim_p, k_p, v_p, f_p, sre_s, sim_s, k_s, v_s, f_s)
```

```python
import functools
import math

import jax
import jax.numpy as jnp
from jax import lax
from jax.experimental import pallas as pl
from jax.experimental.pallas import tpu as pltpu

F32 = jnp.float32
BF16 = jnp.bfloat16
HIGHEST = lax.Precision.HIGHEST

EPS = 1e-6
GROUP_SIZE = 16
STATE_DIM = 64
N_HEADS = 16
PAGE_SIZE = 128
S5_CHUNK = 16
S5_SEGMENTS = 8
V7X_LANES = 128
V7X_VMEM_LIMIT = 56 * 1024 * 1024


def _cparams(sem):
    return pltpu.CompilerParams(dimension_semantics=sem, vmem_limit_bytes=V7X_VMEM_LIMIT)


def _const_spec(shape):
    nd = len(shape)
    return pl.BlockSpec(shape, lambda *_: (0,) * nd, pipeline_mode=pl.Buffered(1))


def _row_spec(tm, width):
    return pl.BlockSpec((tm, width), lambda i: (i, 0))


def _mod_spec(arr, tm):
    if arr.shape[0] == 1:
        return pl.BlockSpec((1, arr.shape[1]), lambda i: (0, 0))
    return pl.BlockSpec((tm, arr.shape[1]), lambda i: (i, 0))


def _rms_hat(x):
    return x * lax.rsqrt(jnp.mean(x * x, axis=-1, keepdims=True) + EPS)


def _mods_kernel(c_ref, w_ref, b_ref, o_ref):
    o_ref[...] = jnp.dot(c_ref[...], w_ref[...], precision=HIGHEST,
                         preferred_element_type=F32) + b_ref[...]


def _mods(c, w3, b3, tn=2048):
    nl, d, n = w3.shape
    mp = c.shape[0]
    return pl.pallas_call(
        _mods_kernel,
        grid=(nl, n // tn),
        in_specs=[pl.BlockSpec((mp, d), lambda l, j: (0, 0)),
                  pl.BlockSpec((None, d, tn), lambda l, j: (l, 0, j)),
                  pl.BlockSpec((None, 1, tn), lambda l, j: (l, 0, j))],
        out_specs=pl.BlockSpec((None, mp, tn), lambda l, j: (l, 0, j)),
        out_shape=jax.ShapeDtypeStruct((nl, mp, n), F32),
        compiler_params=_cparams(("arbitrary", "arbitrary")),
        name="mods",
    )(c, w3, b3)


def _ada_rms_kernel(x_ref, g_ref, sh_ref, sc_ref, o_ref):
    u = (_rms_hat(x_ref[...]) * g_ref[...]) * (1.0 + sc_ref[...]) + sh_ref[...]
    o_ref[...] = u.astype(o_ref.dtype)


def _ada_rms(x, g, sh, sc, tm, out_dtype):
    m, d = x.shape
    return pl.pallas_call(
        _ada_rms_kernel,
        grid=(m // tm,),
        in_specs=[_row_spec(tm, d), _const_spec((1, d)), _mod_spec(sh, tm), _mod_spec(sc, tm)],
        out_specs=_row_spec(tm, d),
        out_shape=jax.ShapeDtypeStruct((m, d), out_dtype),
        compiler_params=_cparams(("arbitrary",)),
        name="ada_rms",
    )(x, g, sh, sc)


def _s5_discretise(lam_re, lam_im, log_dt, b_re, b_im):
    dt = jnp.exp(log_dt.astype(F32))[:, None]
    lr = lam_re.astype(F32)
    li = lam_im.astype(F32)
    mag = jnp.exp(lr * dt)
    ang = li * dt
    ab_re = mag * jnp.cos(ang)
    ab_im = mag * jnp.sin(ang)
    den = lr * lr + li * li
    zr = ((ab_re - 1) * lr + ab_im * li) / den
    zi = (ab_im * lr - (ab_re - 1) * li) / den
    br = b_re.astype(F32)
    bi = b_im.astype(F32)
    bb_re = zr[..., None] * br - zi[..., None] * bi
    bb_im = zr[..., None] * bi + zi[..., None] * br
    return ab_re, ab_im, bb_re, bb_im


def _cpow_table(ar, ai, n):
    def step(carry, _):
        pr, pi = carry
        return (pr * ar - pi * ai, pr * ai + pi * ar), (pr, pi)
    _, (tr, ti) = lax.scan(step, (jnp.ones_like(ar), jnp.zeros_like(ar)), None, length=n)
    return tr, ti


def _pair_lanes(x_re, x_im):
    g = x_re.shape[0]
    def pair(x):
        x = x.reshape((g // 2, 2) + x.shape[1:])
        x = jnp.moveaxis(x, 1, -2)
        return x.reshape(x.shape[:-2] + (2 * x.shape[-1],))
    return jnp.concatenate([pair(x_re), pair(x_im)], axis=-1)


_ein = functools.partial(jnp.einsum, precision=HIGHEST)


def _s5_chunk_weights(ab_re, ab_im, bb_re, bb_im, c_re, c_im, n_local):
    t = S5_CHUNK
    g, p = ab_re.shape
    h = GROUP_SIZE
    cr = c_re.astype(F32)
    ci = c_im.astype(F32)
    pw_re, pw_im = _cpow_table(ab_re, ab_im, t + 1)
    cb_rr = _ein('gop,gph->gpoh', cr, bb_re) - _ein('gop,gph->gpoh', ci, bb_im)
    cb_ii = _ein('gop,gph->gpoh', cr, bb_im) + _ein('gop,gph->gpoh', ci, bb_re)
    kern = _ein('tgp,gpoh->gtoh', pw_re[:t], cb_rr) - _ein('tgp,gpoh->gtoh', pw_im[:t], cb_ii)
    jj = jnp.arange(t)[:, None]
    tt = jnp.arange(t)[None, :]
    tau = tt - jj
    m = kern[:, jnp.clip(tau, 0, t - 1)]
    m = jnp.where((tau >= 0)[None, :, :, None, None], m, 0.0)
    m = m.transpose(0, 1, 4, 2, 3).reshape(g // 2, 2, t * h, t * h)
    rev_re = pw_re[t - 1 - jnp.arange(t)]
    rev_im = pw_im[t - 1 - jnp.arange(t)]
    p_re = _ein('jgp,gph->gjhp', rev_re, bb_re) - _ein('jgp,gph->gjhp', rev_im, bb_im)
    p_im = _ein('jgp,gph->gjhp', rev_re, bb_im) + _ein('jgp,gph->gjhp', rev_im, bb_re)
    p_re = p_re.reshape(g // 2, 2, t * h, p)
    p_im = p_im.reshape(g // 2, 2, t * h, p)
    z = jnp.zeros_like(p_re[:, 0])
    p2 = jnp.concatenate([
        jnp.concatenate([p_re[:, 0], z, p_im[:, 0], z], axis=-1),
        jnp.concatenate([z, p_re[:, 1], z, p_im[:, 1]], axis=-1)], axis=1)
    w_re = _ein('gop,tgp->gpto', cr, pw_re[1:]) - _ein('gop,tgp->gpto', ci, pw_im[1:])
    w_im = _ein('gop,tgp->gpto', cr, pw_im[1:]) + _ein('gop,tgp->gpto', ci, pw_re[1:])
    w_re = w_re.reshape(g // 2, 2, p, t * h)
    w_im = w_im.reshape(g // 2, 2, p, t * h)
    zq = jnp.zeros_like(w_re[:, 0])
    q2 = jnp.concatenate([
        jnp.concatenate([w_re[:, 0], zq], axis=-1),
        jnp.concatenate([zq, w_re[:, 1]], axis=-1),
        jnp.concatenate([-w_im[:, 0], zq], axis=-1),
        jnp.concatenate([zq, -w_im[:, 1]], axis=-1)], axis=1)
    at_re, at_im = pw_re[t], pw_im[t]
    lp_re, lp_im = _cpow_table(at_re, at_im, n_local + 1)
    apow = _pair_lanes(jnp.moveaxis(lp_re[:n_local], 0, 1), jnp.moveaxis(lp_im[:n_local], 0, 1))
    a_chunk = _pair_lanes(at_re[:, None], at_im[:, None])
    a_seg = _pair_lanes(lp_re[n_local][:, None], lp_im[n_local][:, None])
    return m.astype(BF16), p2.astype(BF16), q2.astype(BF16), a_chunk, apow, a_seg


def _s5_prompt_kernel(u_ref, p2_ref, m_ref, q2_ref, ac_ref, apow_ref, aseg_ref,
                      y_ref, fin_ref, g_scr, sp_scr, *, n_pairs, n_local):
    seg = S5_SEGMENTS
    half = 2 * STATE_DIM
    for p in range(n_pairs):
        g_scr[p] = jnp.dot(u_ref[p], p2_ref[p], preferred_element_type=F32)

    ar = [jnp.broadcast_to(ac_ref[p][:, :half], (seg, half)) for p in range(n_pairs)]
    ai = [jnp.broadcast_to(ac_ref[p][:, half:], (seg, half)) for p in range(n_pairs)]

    def scan_step(k, carry):
        row = pl.multiple_of(k * seg, seg)
        out = []
        for p in range(n_pairs):
            sr, si = carry[2 * p], carry[2 * p + 1]
            sp_scr[p, pl.ds(row, seg), :half] = sr
            sp_scr[p, pl.ds(row, seg), half:] = si
            gk = g_scr[p, pl.ds(row, seg), :]
            out.append(ar[p] * sr - ai[p] * si + gk[:, :half])
            out.append(ar[p] * si + ai[p] * sr + gk[:, half:])
        return tuple(out)

    zero = jnp.zeros((seg, half), F32)
    ends = lax.fori_loop(0, n_local, scan_step, (zero,) * (2 * n_pairs))

    for p in range(n_pairs):
        er, ei = ends[2 * p], ends[2 * p + 1]
        sgr = aseg_ref[p][:, :half]
        sgi = aseg_ref[p][:, half:]
        cr = jnp.zeros((1, half), F32)
        ci = jnp.zeros((1, half), F32)
        rows_r, rows_i = [], []
        for s in range(seg):
            rows_r.append(cr)
            rows_i.append(ci)
            cr, ci = (sgr * cr - sgi * ci + er[s:s + 1], sgr * ci + sgi * cr + ei[s:s + 1])
        fin_ref[p] = jnp.concatenate([cr, ci], axis=-1)
        sin_r = jnp.concatenate(rows_r, axis=0)
        sin_i = jnp.concatenate(rows_i, axis=0)

        def fix_step(k, c, p=p, sin_r=sin_r, sin_i=sin_i):
            row = pl.multiple_of(k * seg, seg)
            apk = apow_ref[p, pl.ds(k, 1), :]
            pr = jnp.broadcast_to(apk[:, :half], (seg, half))
            pi = jnp.broadcast_to(apk[:, half:], (seg, half))
            sp_scr[p, pl.ds(row, seg), :half] += pr * sin_r - pi * sin_i
            sp_scr[p, pl.ds(row, seg), half:] += pr * sin_i + pi * sin_r
            return c

        lax.fori_loop(0, n_local, fix_step, 0)

        u = u_ref[p]
        kw = S5_CHUNK * GROUP_SIZE
        y = jnp.dot(sp_scr[p].astype(BF16), q2_ref[p], preferred_element_type=F32)
        y_intra = jnp.concatenate(
            [jnp.dot(u[:, :kw], m_ref[p, 0], preferred_element_type=F32),
             jnp.dot(u[:, kw:], m_ref[p, 1], preferred_element_type=F32)], axis=1)
        y_ref[p] = (y + y_intra).astype(y_ref.dtype)


def _s5_prompt(u_bf16, weights, n_pairs=2):
    m, p2, q2, a_chunk, apow, a_seg = weights
    l, d = u_bf16.shape
    t, h, seg = S5_CHUNK, GROUP_SIZE, S5_SEGMENTS
    gp = d // h // 2
    nc = l // t
    n_local = nc // seg
    kw = t * h
    sw = 4 * STATE_DIM
    ut = u_bf16.reshape(seg, n_local, t, gp, 2, h).transpose(3, 1, 0, 4, 2, 5).reshape(gp, nc, 2 * kw)
    pair_spec = lambda *tail: pl.BlockSpec((n_pairs,) + tail, lambda i: (i,) + (0,) * len(tail))
    yt, fin = pl.pallas_call(
        functools.partial(_s5_prompt_kernel, n_pairs=n_pairs, n_local=n_local),
        grid=(gp // n_pairs,),
        in_specs=[pair_spec(nc, 2 * kw), pair_spec(2 * kw, sw), pair_spec(2, kw, kw),
                  pair_spec(sw, 2 * kw), pair_spec(1, sw), pair_spec(n_local, sw), pair_spec(1, sw)],
        out_specs=[pair_spec(nc, 2 * kw), pair_spec(1, sw)],
        out_shape=[jax.ShapeDtypeStruct((gp, nc, 2 * kw), BF16),
                   jax.ShapeDtypeStruct((gp, 1, sw), F32)],
        scratch_shapes=[pltpu.VMEM((n_pairs, nc, sw), F32), pltpu.VMEM((n_pairs, nc, sw), F32)],
        compiler_params=_cparams(("arbitrary",)),
        name="s5_prompt",
    )(ut, p2, m, q2, a_chunk, apow, a_seg)
    y = yt.reshape(gp, n_local, seg, 2, t, h).transpose(2, 1, 4, 0, 3, 5).reshape(l, d)
    fin = fin.reshape(gp, 2, 2, STATE_DIM)
    fin_re = fin[:, 0].reshape(2 * gp, STATE_DIM)
    fin_im = fin[:, 1].reshape(2 * gp, STATE_DIM)
    return y, fin_re, fin_im


def _s5_step_kernel(u_ref, h_ref, hs_ref, a1_ref, a2_ref, bbt_ref, cq_ref, y_ref, hn_ref):
    bu = jnp.einsum('gbh,ghp->gbp', u_ref[...], bbt_ref[...], precision=HIGHEST,
                    preferred_element_type=F32)
    hn = a1_ref[...] * h_ref[...] + a2_ref[...] * hs_ref[...] + bu
    hn_ref[...] = hn
    y_ref[...] = jnp.einsum('gbp,gph->gbh', hn, cq_ref[...], precision=HIGHEST,
                            preferred_element_type=F32)


def _s5_step(u, h0_re, h0_im, ab_re, ab_im, bb_re, bb_im, c_re, c_im):
    b, d = u.shape
    g, p = ab_re.shape
    h = GROUP_SIZE
    u3 = u.reshape(b, g, h).transpose(1, 0, 2)
    hc = jnp.concatenate([h0_re, h0_im], axis=-1).transpose(1, 0, 2).astype(F32)
    hs = jnp.concatenate([h0_im, h0_re], axis=-1).transpose(1, 0, 2).astype(F32)
    a1 = jnp.concatenate([ab_re, ab_re], axis=-1)[:, None, :]
    a2 = jnp.concatenate([-ab_im, ab_im], axis=-1)[:, None, :]
    bbt = jnp.concatenate([bb_re, bb_im], axis=1).transpose(0, 2, 1)
    cq = jnp.concatenate([c_re.astype(F32), -c_im.astype(F32)], axis=-1).transpose(0, 2, 1)
    y3, hn = pl.pallas_call(
        _s5_step_kernel,
        out_shape=[jax.ShapeDtypeStruct((g, b, h), F32), jax.ShapeDtypeStruct((g, b, 2 * p), F32)],
        compiler_params=pltpu.CompilerParams(vmem_limit_bytes=V7X_VMEM_LIMIT),
        name="s5_step",
    )(u3, hc, hs, a1, a2, bbt, cq)
    y = y3.transpose(1, 0, 2).reshape(b, d)
    hn = hn.transpose(1, 0, 2)
    return y, hn[..., :p], hn[..., p:]


def _glu_kernel(x_ref, y_ref, g_ref, d_ref, sh_ref, sc_ref, gate_ref, w_ref, o_ref):
    x = x_ref[...]
    dm = x.shape[1]
    u = (_rms_hat(x) * g_ref[...]) * (1.0 + sc_ref[...]) + sh_ref[...]
    y = y_ref[...].astype(F32) + d_ref[...] * u
    z = jnp.dot(jax.nn.gelu(y).astype(BF16), w_ref[...], preferred_element_type=F32)
    out = z[:, :dm] * jax.nn.sigmoid(z[:, dm:])
    o_ref[...] = x + gate_ref[...] * out


def _glu(x, y_ssm, g, d_skip, sh, sc, gate, w_glu, tm):
    m, d = x.shape
    return pl.pallas_call(
        _glu_kernel,
        grid=(m // tm,),
        in_specs=[_row_spec(tm, d), _row_spec(tm, d), _const_spec((1, d)), _const_spec((1, d)),
                  _mod_spec(sh, tm), _mod_spec(sc, tm), _mod_spec(gate, tm), _const_spec((d, 2 * d))],
        out_specs=_row_spec(tm, d),
        out_shape=jax.ShapeDtypeStruct((m, d), F32),
        compiler_params=_cparams(("arbitrary",)),
        name="glu",
    )(x, y_ssm, g, d_skip, sh, sc, gate, w_glu)


def _mlp_kernel(x_ref, g_ref, sh_ref, sc_ref, gate_ref, w1_ref, w2_ref, o_ref, *, ff_chunk):
    x = x_ref[...]
    xn = ((_rms_hat(x) * g_ref[...]) * (1.0 + sc_ref[...]) + sh_ref[...]).astype(BF16)
    acc = jnp.zeros(x.shape, F32)
    for c in range(w1_ref.shape[1] // ff_chunk):
        a = jnp.dot(xn, w1_ref[:, c * ff_chunk:(c + 1) * ff_chunk], preferred_element_type=F32)
        a = jnp.square(jnp.maximum(a, 0.0)).astype(BF16)
        acc = acc + jnp.dot(a, w2_ref[c * ff_chunk:(c + 1) * ff_chunk, :], preferred_element_type=F32)
    o_ref[...] = x + gate_ref[...] * acc


def _mlp(x, g, sh, sc, gate, w1, w2, tm, ff_chunk=1024):
    m, d = x.shape
    dff = w1.shape[1]
    return pl.pallas_call(
        functools.partial(_mlp_kernel, ff_chunk=ff_chunk),
        grid=(m // tm,),
        in_specs=[_row_spec(tm, d), _const_spec((1, d)), _mod_spec(sh, tm), _mod_spec(sc, tm),
                  _mod_spec(gate, tm), _const_spec((d, dff)), _const_spec((dff, d))],
        out_specs=_row_spec(tm, d),
        out_shape=jax.ShapeDtypeStruct((m, d), F32),
        compiler_params=_cparams(("arbitrary",)),
        name="mlp",
    )(x, g, sh, sc, gate, w1, w2)


def _head_norm(t, gain, e_ref, et_ref):
    msq = jnp.dot((t * t).astype(BF16), e_ref[...], preferred_element_type=F32)
    r = lax.rsqrt(msq + EPS)
    r_hi = r.astype(BF16)
    r_lo = (r - r_hi.astype(F32)).astype(BF16)
    r_full = (jnp.dot(r_hi, et_ref[...], preferred_element_type=F32)
              + jnp.dot(r_lo, et_ref[...], preferred_element_type=F32))
    return (t * r_full) * gain


def _qkv_kernel(h_ref, gkv_ref, shkv_ref, sckv_ref, gq_ref, shq_ref, scq_ref,
                wkvf_ref, bf_ref, wq_ref, kn_ref, qn_ref, e_ref, et_ref,
                k_ref, v_ref, lf_ref, q_ref, *, q_scale):
    x = h_ref[...]
    dm = x.shape[1]
    xh = _rms_hat(x)
    xkv = ((xh * gkv_ref[...]) * (1.0 + sckv_ref[...]) + shkv_ref[...]).astype(BF16)
    xq = ((xh * gq_ref[...]) * (1.0 + scq_ref[...]) + shq_ref[...]).astype(BF16)
    proj = jnp.dot(xkv, wkvf_ref[...], preferred_element_type=F32)
    k_ref[...] = _head_norm(proj[:, :dm], kn_ref[...], e_ref, et_ref)
    v_ref[...] = proj[:, dm:2 * dm]
    lf = jax.nn.log_sigmoid(proj[:, 2 * dm:] + bf_ref[...])
    lf_ref[...] = lf[:, :lf_ref.shape[1]]
    q = _head_norm(jnp.dot(xq, wq_ref[...], preferred_element_type=F32), qn_ref[...], e_ref, et_ref)
    q_ref[...] = (q * q_scale).astype(q_ref.dtype)


def _qkv(h, g_kv, sh_kv, sc_kv, g_q, sh_q, sc_q, w_kvf, b_f, w_q, k_norm, q_norm, e, et, tm):
    m, d = h.shape
    nkvf = w_kvf.shape[1]
    hd = d // N_HEADS
    return pl.pallas_call(
        functools.partial(_qkv_kernel, q_scale=1.0 / math.sqrt(hd)),
        grid=(m // tm,),
        in_specs=[_row_spec(tm, d),
                  _const_spec((1, d)), _mod_spec(sh_kv, tm), _mod_spec(sc_kv, tm),
                  _const_spec((1, d)), _mod_spec(sh_q, tm), _mod_spec(sc_q, tm),
                  _const_spec((d, nkvf)), _const_spec((1, V7X_LANES)), _const_spec((d, d)),
                  _const_spec((1, d)), _const_spec((1, d)),
                  _const_spec((d, V7X_LANES)), _const_spec((V7X_LANES, d))],
        out_specs=[_row_spec(tm, d), _row_spec(tm, d), _row_spec(tm, N_HEADS), _row_spec(tm, d)],
        out_shape=[jax.ShapeDtypeStruct((m, d), F32), jax.ShapeDtypeStruct((m, d), F32),
                   jax.ShapeDtypeStruct((m, N_HEADS), F32), jax.ShapeDtypeStruct((m, d), BF16)],
        compiler_params=_cparams(("arbitrary",)),
        name="qkv",
    )(h, g_kv, sh_kv, sc_kv, g_q, sh_q, sc_q, w_kvf, b_f, w_q, k_norm, q_norm, e, et)


def _cumsum_kernel(x_ref, o_ref, carry_ref):
    @pl.when(pl.program_id(0) == 0)
    def _():
        carry_ref[...] = jnp.zeros_like(carry_ref)
    tm = x_ref.shape[0]
    tri = (lax.broadcasted_iota(jnp.int32, (tm, tm), 1)
           <= lax.broadcasted_iota(jnp.int32, (tm, tm), 0)).astype(F32)
    f = jnp.dot(tri, x_ref[...], precision=HIGHEST, preferred_element_type=F32) + carry_ref[...]
    o_ref[...] = f
    carry_ref[...] = f[tm - 1:tm, :]


def _cumsum_rows(x, tm):
    m, n = x.shape
    return pl.pallas_call(
        _cumsum_kernel,
        grid=(m // tm,),
        in_specs=[_row_spec(tm, n)],
        out_specs=_row_spec(tm, n),
        out_shape=jax.ShapeDtypeStruct((m, n), F32),
        scratch_shapes=[pltpu.VMEM((1, n), F32)],
        compiler_params=_cparams(("arbitrary",)),
        name="logf_cumsum",
    )(x)


ATTN_AUG = 3
ATTN_VROWS = 80


def _attn_kernel(q_ref, k_ref, v_ref, o_ref, m_scr, acc_scr, *, hd):
    i = pl.program_id(1)
    q_t = q_ref[...]
    tq = q_t.shape[1]
    m_scr[...] = jnp.full(m_scr.shape, -jnp.inf, F32)
    acc_scr[...] = jnp.zeros(acc_scr.shape, F32)

    def tile(j, masked):
        s_t = jnp.dot(k_ref[j], q_t, preferred_element_type=F32)
        if masked:
            key = lax.broadcasted_iota(jnp.int32, s_t.shape, 0)
            qry = lax.broadcasted_iota(jnp.int32, s_t.shape, 1)
            s_t = jnp.where(key <= qry, s_t, -jnp.inf)
        m_old = m_scr[...]
        m_new = jnp.maximum(m_old, jnp.max(s_t, axis=0, keepdims=True))
        alpha = jnp.exp(m_old - m_new)
        p_t = jnp.exp(s_t - m_new).astype(BF16)
        acc_scr[...] = alpha * acc_scr[...] + jnp.dot(v_ref[j], p_t, preferred_element_type=F32)
        m_scr[...] = m_new

    def body(j, c):
        tile(j, False)
        return c

    lax.fori_loop(0, i, body, 0)
    tile(i, True)
    acc = acc_scr[...]
    o_ref[...] = (acc[:hd] / acc[hd:hd + 1]).astype(o_ref.dtype)


def _trunc_bf16(x):
    bits = lax.bitcast_convert_type(x, jnp.uint32) & jnp.uint32(0xFFFF0000)
    return lax.bitcast_convert_type(bits, F32)


def _split3(x):
    hi = _trunc_bf16(x)
    r = x - hi
    mid = _trunc_bf16(r)
    lo = _trunc_bf16(r - mid)
    return hi.astype(BF16), mid.astype(BF16), lo.astype(BF16)


def _attn_prompt(q_bf16, k, v, fcum, tq):
    l, d = k.shape
    nh = N_HEADS
    hd = d // nh
    nt = l // tq
    q_t = q_bf16.reshape(l, nh, hd).transpose(1, 2, 0)
    q_aug = jnp.concatenate([q_t, jnp.ones((nh, ATTN_AUG, l), BF16),
                             jnp.zeros((nh, V7X_LANES - hd - ATTN_AUG, l), BF16)], axis=1)
    kh = k.astype(BF16).reshape(l, nh, hd).transpose(1, 0, 2)
    f3 = jnp.stack(_split3(-fcum.T), axis=-1)
    k_aug = jnp.concatenate([kh, f3, jnp.zeros((nh, l, V7X_LANES - hd - ATTN_AUG), BF16)], axis=-1)
    k_aug = k_aug.reshape(nh, nt, tq, V7X_LANES)
    v_t = v.astype(BF16).reshape(l, nh, hd).transpose(1, 2, 0)
    v_aug = jnp.concatenate([v_t, jnp.ones((nh, 1, l), BF16),
                             jnp.zeros((nh, ATTN_VROWS - hd - 1, l), BF16)], axis=1)
    v_aug = v_aug.reshape(nh, ATTN_VROWS, nt, tq).transpose(0, 2, 1, 3)
    o_t = pl.pallas_call(
        functools.partial(_attn_kernel, hd=hd),
        grid=(nh, nt),
        in_specs=[pl.BlockSpec((None, V7X_LANES, tq), lambda h, i: (h, 0, i)),
                  pl.BlockSpec((None, nt, tq, V7X_LANES), lambda h, i: (h, 0, 0, 0)),
                  pl.BlockSpec((None, nt, ATTN_VROWS, tq), lambda h, i: (h, 0, 0, 0))],
        out_specs=pl.BlockSpec((None, hd, tq), lambda h, i: (h, 0, i)),
        out_shape=jax.ShapeDtypeStruct((nh, hd, l), BF16),
        scratch_shapes=[pltpu.VMEM((1, tq), F32), pltpu.VMEM((ATTN_VROWS, tq), F32)],
        compiler_params=_cparams(("arbitrary", "arbitrary")),
        name="attn_prompt",
    )(q_aug, k_aug, v_aug)
    return o_t.transpose(2, 0, 1).reshape(l, d)


def _attn_paged_kernel(pt_ref, qbd_ref, knew_ref, vnew_ref, lfnew_ref, *refs, pages_per_step, hd):
    del pt_ref
    pps = pages_per_step
    k_refs = refs[:pps]
    v_refs = refs[pps:2 * pps]
    lf_refs = refs[2 * pps:3 * pps]
    o_ref, m_scr, l_scr, f_scr, acc_scr = refs[3 * pps:]
    nh = N_HEADS
    step = pl.program_id(1)

    @pl.when(step == 0)
    def _():
        m_scr[...] = jnp.full(m_scr.shape, -jnp.inf, F32)
        l_scr[...] = jnp.zeros(l_scr.shape, F32)
        f_scr[...] = jnp.zeros(f_scr.shape, F32)
        acc_scr[...] = jnp.zeros(acc_scr.shape, F32)

    qbd = qbd_ref[...]
    qbd_b = qbd.astype(BF16)
    ps = k_refs[0].shape[0]
    tri = (lax.broadcasted_iota(jnp.int32, (ps, ps), 1)
           <= lax.broadcasted_iota(jnp.int32, (ps, ps), 0)).astype(F32)
    for r in range(pps):
        kb = k_refs[r][...].astype(BF16)
        vb = v_refs[r][...].astype(BF16)
        lf = lf_refs[r][...]
        lf_pad = jnp.concatenate([lf, jnp.zeros((ps, V7X_LANES - nh), F32)], axis=1)
        f_page = jnp.dot(tri, lf_pad, precision=HIGHEST, preferred_element_type=F32)
        f_t = f_page.T[:nh] + f_scr[...]
        s_t = lax.dot_general(qbd_b, kb, (((1,), (1,)), ((), ())), preferred_element_type=F32)
        z = s_t - f_t
        m_old = m_scr[...]
        m_new = jnp.maximum(m_old, jnp.max(z, axis=1, keepdims=True))
        alpha = jnp.exp(m_old - m_new)
        p = jnp.exp(z - m_new)
        l_scr[...] = alpha * l_scr[...] + jnp.sum(p, axis=1, keepdims=True)
        acc_scr[...] = alpha * acc_scr[...] + jnp.dot(p.astype(BF16), vb, preferred_element_type=F32)
        m_scr[...] = m_new
        f_scr[...] = f_t[:, ps - 1:ps]

    @pl.when(step == pl.num_programs(1) - 1)
    def _():
        f_new = f_scr[...] + lfnew_ref[...]
        s_new = jnp.sum(qbd * knew_ref[...], axis=1, keepdims=True)
        z = s_new - f_new
        m_old = m_scr[...]
        m_new = jnp.maximum(m_old, z)
        alpha = jnp.exp(m_old - m_new)
        p = jnp.exp(z - m_new)
        l = alpha * l_scr[...] + p
        acc = alpha * acc_scr[...] + p * vnew_ref[...]
        head_of_col = lax.broadcasted_iota(jnp.int32, acc.shape, 1) // hd
        row = lax.broadcasted_iota(jnp.int32, acc.shape, 0)
        o = jnp.where(head_of_col == row, acc / l, 0.0)
        o_ref[...] = jnp.sum(o, axis=0, keepdims=True).astype(o_ref.dtype)


def _attn_paged(q_bf16, k_new, v_new, lf_new, cache_k, cache_v, cache_logf, page_table,
                pages_per_step=4):
    b, d = k_new.shape
    nh = N_HEADS
    hd = d // nh
    n_pages = page_table.shape[1]
    n_phys, ps = cache_k.shape[0], cache_k.shape[1]
    pps = pages_per_step
    ck = cache_k.reshape(n_phys, ps, d)
    cv = cache_v.reshape(n_phys, ps, d)
    eye = jnp.repeat(jnp.eye(nh, dtype=F32), hd, axis=1)
    qbd = q_bf16.astype(F32)[:, None, :] * eye[None]
    pt = page_table.reshape(-1).astype(jnp.int32)

    def page_map(r):
        return lambda s, g, pt_ref: (pt_ref[s * n_pages + g * pps + r], 0, 0)

    seq3 = lambda width: pl.BlockSpec((None, 1, width), lambda s, g, pt_ref: (s, 0, 0))
    in_specs = ([pl.BlockSpec((None, nh, d), lambda s, g, pt_ref: (s, 0, 0)), seq3(d), seq3(d),
                 pl.BlockSpec((None, nh, 1), lambda s, g, pt_ref: (s, 0, 0))]
                + [pl.BlockSpec((None, ps, d), page_map(r)) for r in range(pps)]
                + [pl.BlockSpec((None, ps, d), page_map(r)) for r in range(pps)]
                + [pl.BlockSpec((None, ps, nh), page_map(r)) for r in range(pps)])
    out = pl.pallas_call(
        functools.partial(_attn_paged_kernel, pages_per_step=pps, hd=hd),
        grid_spec=pltpu.PrefetchScalarGridSpec(
            num_scalar_prefetch=1,
            grid=(b, n_pages // pps),
            in_specs=in_specs,
            out_specs=pl.BlockSpec((None, 1, d), lambda s, g, pt_ref: (s, 0, 0)),
            scratch_shapes=[pltpu.VMEM((nh, 1), F32), pltpu.VMEM((nh, 1), F32),
                            pltpu.VMEM((nh, 1), F32), pltpu.VMEM((nh, d), F32)]),
        out_shape=jax.ShapeDtypeStruct((b, 1, d), BF16),
        compiler_params=_cparams(("arbitrary", "arbitrary")),
        name="attn_paged",
    )(pt, qbd, k_new.reshape(b, 1, d), v_new.reshape(b, 1, d), lf_new.reshape(b, nh, 1),
      *([ck] * pps), *([cv] * pps), *([cache_logf] * pps))
    return out.reshape(b, d)


def _oproj_kernel(o_ref, h_ref, gate_ref, w_ref, out_ref):
    out_ref[...] = h_ref[...] + gate_ref[...] * jnp.dot(o_ref[...], w_ref[...],
                                                        preferred_element_type=F32)


def _oproj(o, h, gate, w_o, tm):
    m, d = h.shape
    return pl.pallas_call(
        _oproj_kernel,
        grid=(m // tm,),
        in_specs=[_row_spec(tm, d), _row_spec(tm, d), _mod_spec(gate, tm), _const_spec((d, d))],
        out_specs=_row_spec(tm, d),
        out_shape=jax.ShapeDtypeStruct((m, d), F32),
        compiler_params=_cparams(("arbitrary",)),
        name="oproj",
    )(o, h, gate, w_o)


def _row_tile(m):
    return min(m, 512)


def _layer_mods(mod_all, kv_all, rows, d):
    per_layer = [[mod_all[layer, rows, i * d:(i + 1) * d] for i in range(6)]
                 for layer in range(mod_all.shape[0])]
    kv = [kv_all[0, rows, i * d:(i + 1) * d] for i in range(2)]
    return per_layer, kv


def kernel(x_prompt, x_sample, c_prompt, c_sample, state_ssm_re, state_ssm_im, cache_k, cache_v, cache_logf, page_table, ada_w, ada_b, g_mix, g_mlp, mlp_w1, mlp_w2, ssm_lam_re, ssm_lam_im, ssm_log_dt, ssm_b_re, ssm_b_im, ssm_c_re, ssm_c_im, ssm_d, w_glu, kv_ada_w, kv_ada_b, g_kv, w_kvf, b_f, k_norm, w_q, q_norm, w_o):
    bp, l, d = x_prompt.shape
    bs = x_sample.shape[0]
    nh = N_HEADS
    hd = d // nh
    assert bp == 1 and x_sample.shape[1] == 1 and ada_w.shape[0] == 2
    assert l % (S5_CHUNK * S5_SEGMENTS * 8) == 0

    row = lambda a: a.reshape(1, -1).astype(F32)
    w_glu_b = w_glu[0].astype(BF16)
    w1_b = mlp_w1.astype(BF16)
    w2_b = mlp_w2.astype(BF16)
    w_q_b = w_q[0].astype(BF16)
    w_o_b = w_o[0].astype(BF16)
    w_kvf_b = jnp.concatenate(
        [w_kvf, jnp.zeros((d, V7X_LANES - nh), w_kvf.dtype)], axis=1).astype(BF16)
    b_f_p = jnp.concatenate([b_f.astype(F32), jnp.zeros((V7X_LANES - nh,), F32)]).reshape(1, -1)
    head_of = jnp.arange(d) // hd
    e = (head_of[:, None] == jnp.arange(V7X_LANES)[None, :]).astype(F32)
    e_mean = (e / hd).astype(BF16)
    e_t = e.T.astype(BF16)
    k_norm_full = jnp.tile(k_norm.astype(F32), nh).reshape(1, d)
    q_norm_full = jnp.tile(q_norm[0].astype(F32), nh).reshape(1, d)

    ab_re, ab_im, bb_re, bb_im = _s5_discretise(ssm_lam_re[0], ssm_lam_im[0], ssm_log_dt[0],
                                                ssm_b_re[0], ssm_b_im[0])
    n_local = l // S5_CHUNK // S5_SEGMENTS
    s5_w = _s5_chunk_weights(ab_re, ab_im, bb_re, bb_im, ssm_c_re[0], ssm_c_im[0], n_local)

    n_c = bp + bs
    c_rows = -(-n_c // 8) * 8
    c_all = jnp.concatenate([c_prompt, c_sample, jnp.zeros((c_rows - n_c, d), F32)], axis=0)
    mod_all = _mods(c_all, ada_w, ada_b.reshape(ada_b.shape[0], 1, -1))
    kv_all = _mods(c_all, kv_ada_w[None], kv_ada_b.reshape(1, 1, -1))

    def trunk(x, rows, s5_fn, attend):
        m = x.shape[0]
        tm = _row_tile(m)
        (l0, l1), (sh_kv, sc_kv) = _layer_mods(mod_all, kv_all, rows, d)
        sh_m, sc_m, g_m, sh_f, sc_f, g_f = l0
        y_ssm, st_re, st_im = s5_fn(x, row(g_mix[0]), sh_m, sc_m, tm)
        h = _glu(x, y_ssm, row(g_mix[0]), row(ssm_d[0]), sh_m, sc_m, g_m, w_glu_b, tm)
        h = _mlp(h, row(g_mlp[0]), sh_f, sc_f, g_f, w1_b[0], w2_b[0], tm)
        sh_m, sc_m, g_m, sh_f, sc_f, g_f = l1
        k, v, lf, q = _qkv(h, row(g_kv), sh_kv, sc_kv, row(g_mix[1]), sh_m, sc_m,
                           w_kvf_b, b_f_p, w_q_b, k_norm_full, q_norm_full, e_mean, e_t, tm)
        o = attend(q, k, v, lf)
        h = _oproj(o, h, g_m, w_o_b, tm)
        h = _mlp(h, row(g_mlp[1]), sh_f, sc_f, g_f, w1_b[1], w2_b[1], tm)
        return h, st_re, st_im, k, v, lf

    def s5_prompt(x, g, sh, sc, tm):
        u = _ada_rms(x, g, sh, sc, tm, BF16)
        return _s5_prompt(u, s5_w)

    def attend_prompt(q, k, v, lf):
        fcum = _cumsum_rows(lf, _row_tile(l))
        return _attn_prompt(q, k, v, fcum, min(l, 512))

    y_p, sre_p, sim_p, k_p, v_p, f_p = trunk(x_prompt[0], slice(0, 1), s5_prompt, attend_prompt)

    def s5_sample(x, g, sh, sc, tm):
        u = _ada_rms(x, g, sh, sc, tm, F32)
        return _s5_step(u, state_ssm_re[0], state_ssm_im[0], ab_re, ab_im, bb_re, bb_im,
                        ssm_c_re[0], ssm_c_im[0])

    def attend_sample(q, k, v, lf):
        return _attn_paged(q, k, v, lf, cache_k, cache_v, cache_logf, page_table)

    y_s, sre_s, sim_s, k_s, v_s, f_s = trunk(x_sample[:, 0], slice(bp, bp + bs), s5_sample,
                                             attend_sample)

    sdt = state_ssm_re.dtype
    return (y_p[None], y_s[:, None],
            sre_p[None, None].astype(sdt), sim_p[None, None].astype(sdt),
            k_p.reshape(1, l, nh, hd), v_p.reshape(1, l, nh, hd), f_p[None],
            sre_s[None].astype(sdt), sim_s[None].astype(sdt),
            k_s.reshape(bs, 1, nh, hd), v_s.reshape(bs, 1, nh, hd), f_s[:, None])
```

```python
import functools
import math

import jax
import jax.numpy as jnp
from jax import lax
from jax.experimental import pallas as pl
from jax.experimental.pallas import tpu as pltpu

F32 = jnp.float32
BF16 = jnp.bfloat16
HIGHEST = lax.Precision.HIGHEST

EPS = 1e-6
GROUP_SIZE = 16
STATE_DIM = 64
N_HEADS = 16
PAGE_SIZE = 128
S5_CHUNK = 16
S5_SEGMENTS = 8
V7X_LANES = 128
V7X_VMEM_LIMIT = 56 * 1024 * 1024


def _cparams(sem):
    return pltpu.CompilerParams(dimension_semantics=sem, vmem_limit_bytes=V7X_VMEM_LIMIT)


def _const_spec(shape):
    nd = len(shape)
    return pl.BlockSpec(shape, lambda *_: (0,) * nd, pipeline_mode=pl.Buffered(1))


def _row_spec(tm, width):
    return pl.BlockSpec((tm, width), lambda i: (i, 0))


def _mod_spec(arr, tm):
    if arr.shape[0] == 1:
        return pl.BlockSpec((1, arr.shape[1]), lambda i: (0, 0))
    return pl.BlockSpec((tm, arr.shape[1]), lambda i: (i, 0))


def _rms_hat(x):
    return x * lax.rsqrt(jnp.mean(x * x, axis=-1, keepdims=True) + EPS)


def _mods_kernel(c_ref, w_ref, b_ref, o_ref):
    o_ref[...] = jnp.dot(c_ref[...], w_ref[...], precision=HIGHEST,
                         preferred_element_type=F32) + b_ref[...]


def _mods(c, w3, b3, tn=2048):
    nl, d, n = w3.shape
    mp = c.shape[0]
    return pl.pallas_call(
        _mods_kernel,
        grid=(nl, n // tn),
        in_specs=[pl.BlockSpec((mp, d), lambda l, j: (0, 0)),
                  pl.BlockSpec((None, d, tn), lambda l, j: (l, 0, j)),
                  pl.BlockSpec((None, 1, tn), lambda l, j: (l, 0, j))],
        out_specs=pl.BlockSpec((None, mp, tn), lambda l, j: (l, 0, j)),
        out_shape=jax.ShapeDtypeStruct((nl, mp, n), F32),
        compiler_params=_cparams(("arbitrary", "arbitrary")),
        name="mods",
    )(c, w3, b3)


def _ada_rms_kernel(x_ref, g_ref, sh_ref, sc_ref, o_ref):
    u = (_rms_hat(x_ref[...]) * g_ref[...]) * (1.0 + sc_ref[...]) + sh_ref[...]
    o_ref[...] = u.astype(o_ref.dtype)


def _ada_rms(x, g, sh, sc, tm, out_dtype):
    m, d = x.shape
    return pl.pallas_call(
        _ada_rms_kernel,
        grid=(m // tm,),
        in_specs=[_row_spec(tm, d), _const_spec((1, d)), _mod_spec(sh, tm), _mod_spec(sc, tm)],
        out_specs=_row_spec(tm, d),
        out_shape=jax.ShapeDtypeStruct((m, d), out_dtype),
        compiler_params=_cparams(("arbitrary",)),
        name="ada_rms",
    )(x, g, sh, sc)


def _s5_discretise(lam_re, lam_im, log_dt, b_re, b_im):
    dt = jnp.exp(log_dt.astype(F32))[:, None]
    lr = lam_re.astype(F32)
    li = lam_im.astype(F32)
    mag = jnp.exp(lr * dt)
    ang = li * dt
    ab_re = mag * jnp.cos(ang)
    ab_im = mag * jnp.sin(ang)
    den = lr * lr + li * li
    zr = ((ab_re - 1) * lr + ab_im * li) / den
    zi = (ab_im * lr - (ab_re - 1) * li) / den
    br = b_re.astype(F32)
    bi = b_im.astype(F32)
    bb_re = zr[..., None] * br - zi[..., None] * bi
    bb_im = zr[..., None] * bi + zi[..., None] * br
    return ab_re, ab_im, bb_re, bb_im


def _cpow_table(ar, ai, n):
    def step(carry, _):
        pr, pi = carry
        return (pr * ar - pi * ai, pr * ai + pi * ar), (pr, pi)
    _, (tr, ti) = lax.scan(step, (jnp.ones_like(ar), jnp.zeros_like(ar)), None, length=n)
    return tr, ti


def _pair_lanes(x_re, x_im):
    g = x_re.shape[0]
    def pair(x):
        x = x.reshape((g // 2, 2) + x.shape[1:])
        x = jnp.moveaxis(x, 1, -2)
        return x.reshape(x.shape[:-2] + (2 * x.shape[-1],))
    return jnp.concatenate([pair(x_re), pair(x_im)], axis=-1)


_ein = functools.partial(jnp.einsum, precision=HIGHEST)


def _s5_chunk_weights(ab_re, ab_im, bb_re, bb_im, c_re, c_im, n_local):
    t = S5_CHUNK
    g, p = ab_re.shape
    h = GROUP_SIZE
    cr = c_re.astype(F32)
    ci = c_im.astype(F32)
    pw_re, pw_im = _cpow_table(ab_re, ab_im, t + 1)
    cb_rr = _ein('gop,gph->gpoh', cr, bb_re) - _ein('gop,gph->gpoh', ci, bb_im)
    cb_ii = _ein('gop,gph->gpoh', cr, bb_im) + _ein('gop,gph->gpoh', ci, bb_re)
    kern = _ein('tgp,gpoh->gtoh', pw_re[:t], cb_rr) - _ein('tgp,gpoh->gtoh', pw_im[:t], cb_ii)
    jj = jnp.arange(t)[:, None]
    tt = jnp.arange(t)[None, :]
    tau = tt - jj
    m = kern[:, jnp.clip(tau, 0, t - 1)]
    m = jnp.where((tau >= 0)[None, :, :, None, None], m, 0.0)
    m = m.transpose(0, 1, 4, 2, 3).reshape(g // 2, 2, t * h, t * h)
    rev_re = pw_re[t - 1 - jnp.arange(t)]
    rev_im = pw_im[t - 1 - jnp.arange(t)]
    p_re = _ein('jgp,gph->gjhp', rev_re, bb_re) - _ein('jgp,gph->gjhp', rev_im, bb_im)
    p_im = _ein('jgp,gph->gjhp', rev_re, bb_im) + _ein('jgp,gph->gjhp', rev_im, bb_re)
    p_re = p_re.reshape(g // 2, 2, t * h, p)
    p_im = p_im.reshape(g // 2, 2, t * h, p)
    z = jnp.zeros_like(p_re[:, 0])
    p2 = jnp.concatenate([
        jnp.concatenate([p_re[:, 0], z, p_im[:, 0], z], axis=-1),
        jnp.concatenate([z, p_re[:, 1], z, p_im[:, 1]], axis=-1)], axis=1)
    w_re = _ein('gop,tgp->gpto', cr, pw_re[1:]) - _ein('gop,tgp->gpto', ci, pw_im[1:])
    w_im = _ein('gop,tgp->gpto', cr, pw_im[1:]) + _ein('gop,tgp->gpto', ci, pw_re[1:])
    w_re = w_re.reshape(g // 2, 2, p, t * h)
    w_im = w_im.reshape(g // 2, 2, p, t * h)
    zq = jnp.zeros_like(w_re[:, 0])
    q2 = jnp.concatenate([
        jnp.concatenate([w_re[:, 0], zq], axis=-1),
        jnp.concatenate([zq, w_re[:, 1]], axis=-1),
        jnp.concatenate([-w_im[:, 0], zq], axis=-1),
        jnp.concatenate([zq, -w_im[:, 1]], axis=-1)], axis=1)
    at_re, at_im = pw_re[t], pw_im[t]
    lp_re, lp_im = _cpow_table(at_re, at_im, n_local + 1)
    apow = _pair_lanes(jnp.moveaxis(lp_re[:n_local], 0, 1), jnp.moveaxis(lp_im[:n_local], 0, 1))
    a_chunk = _pair_lanes(at_re[:, None], at_im[:, None])
    a_seg = _pair_lanes(lp_re[n_local][:, None], lp_im[n_local][:, None])
    return m.astype(BF16), p2.astype(BF16), q2.astype(BF16), a_chunk, apow, a_seg


def _s5_prompt_kernel(u_ref, p2_ref, m_ref, q2_ref, ac_ref, apow_ref, aseg_ref,
                      y_ref, fin_ref, g_scr, sp_scr, *, n_pairs, n_local):
    seg = S5_SEGMENTS
    half = 2 * STATE_DIM
    for p in range(n_pairs):
        g_scr[p] = jnp.dot(u_ref[p], p2_ref[p], preferred_element_type=F32)

    ar = [jnp.broadcast_to(ac_ref[p][:, :half], (seg, half)) for p in range(n_pairs)]
    ai = [jnp.broadcast_to(ac_ref[p][:, half:], (seg, half)) for p in range(n_pairs)]

    def scan_step(k, carry):
        row = pl.multiple_of(k * seg, seg)
        out = []
        for p in range(n_pairs):
            sr, si = carry[2 * p], carry[2 * p + 1]
            sp_scr[p, pl.ds(row, seg), :half] = sr
            sp_scr[p, pl.ds(row, seg), half:] = si
            gk = g_scr[p, pl.ds(row, seg), :]
            out.append(ar[p] * sr - ai[p] * si + gk[:, :half])
            out.append(ar[p] * si + ai[p] * sr + gk[:, half:])
        return tuple(out)

    zero = jnp.zeros((seg, half), F32)
    ends = lax.fori_loop(0, n_local, scan_step, (zero,) * (2 * n_pairs))

    for p in range(n_pairs):
        er, ei = ends[2 * p], ends[2 * p + 1]
        sgr = aseg_ref[p][:, :half]
        sgi = aseg_ref[p][:, half:]
        cr = jnp.zeros((1, half), F32)
        ci = jnp.zeros((1, half), F32)
        rows_r, rows_i = [], []
        for s in range(seg):
            rows_r.append(cr)
            rows_i.append(ci)
            cr, ci = (sgr * cr - sgi * ci + er[s:s + 1], sgr * ci + sgi * cr + ei[s:s + 1])
        fin_ref[p] = jnp.concatenate([cr, ci], axis=-1)
        sin_r = jnp.concatenate(rows_r, axis=0)
        sin_i = jnp.concatenate(rows_i, axis=0)

        def fix_step(k, c, p=p, sin_r=sin_r, sin_i=sin_i):
            row = pl.multiple_of(k * seg, seg)
            apk = apow_ref[p, pl.ds(k, 1), :]
            pr = jnp.broadcast_to(apk[:, :half], (seg, half))
            pi = jnp.broadcast_to(apk[:, half:], (seg, half))
            sp_scr[p, pl.ds(row, seg), :half] += pr * sin_r - pi * sin_i
            sp_scr[p, pl.ds(row, seg), half:] += pr * sin_i + pi * sin_r
            return c

        lax.fori_loop(0, n_local, fix_step, 0)

        u = u_ref[p]
        kw = S5_CHUNK * GROUP_SIZE
        y = jnp.dot(sp_scr[p].astype(BF16), q2_ref[p], preferred_element_type=F32)
        y_intra = jnp.concatenate(
            [jnp.dot(u[:, :kw], m_ref[p, 0], preferred_element_type=F32),
             jnp.dot(u[:, kw:], m_ref[p, 1], preferred_element_type=F32)], axis=1)
        y_ref[p] = (y + y_intra).astype(y_ref.dtype)


def _s5_prompt(u_bf16, weights, n_pairs=2):
    m, p2, q2, a_chunk, apow, a_seg = weights
    l, d = u_bf16.shape
    t, h, seg = S5_CHUNK, GROUP_SIZE, S5_SEGMENTS
    gp = d // h // 2
    nc = l // t
    n_local = nc // seg
    kw = t * h
    sw = 4 * STATE_DIM
    ut = u_bf16.reshape(seg, n_local, t, gp, 2, h).transpose(3, 1, 0, 4, 2, 5).reshape(gp, nc, 2 * kw)
    pair_spec = lambda *tail: pl.BlockSpec((n_pairs,) + tail, lambda i: (i,) + (0,) * len(tail))
    yt, fin = pl.pallas_call(
        functools.partial(_s5_prompt_kernel, n_pairs=n_pairs, n_local=n_local),
        grid=(gp // n_pairs,),
        in_specs=[pair_spec(nc, 2 * kw), pair_spec(2 * kw, sw), pair_spec(2, kw, kw),
                  pair_spec(sw, 2 * kw), pair_spec(1, sw), pair_spec(n_local, sw), pair_spec(1, sw)],
        out_specs=[pair_spec(nc, 2 * kw), pair_spec(1, sw)],
        out_shape=[jax.ShapeDtypeStruct((gp, nc, 2 * kw), BF16),
                   jax.ShapeDtypeStruct((gp, 1, sw), F32)],
        scratch_shapes=[pltpu.VMEM((n_pairs, nc, sw), F32), pltpu.VMEM((n_pairs, nc, sw), F32)],
        compiler_params=_cparams(("arbitrary",)),
        name="s5_prompt",
    )(ut, p2, m, q2, a_chunk, apow, a_seg)
    y = yt.reshape(gp, n_local, seg, 2, t, h).transpose(2, 1, 4, 0, 3, 5).reshape(l, d)
    fin = fin.reshape(gp, 2, 2, STATE_DIM)
    fin_re = fin[:, 0].reshape(2 * gp, STATE_DIM)
    fin_im = fin[:, 1].reshape(2 * gp, STATE_DIM)
    return y, fin_re, fin_im


def _s5_step_kernel(u_ref, h_ref, hs_ref, a1_ref, a2_ref, bbt_ref, cq_ref, y_ref, hn_ref):
    bu = jnp.einsum('gbh,ghp->gbp', u_ref[...], bbt_ref[...], precision=HIGHEST,
                    preferred_element_type=F32)
    hn = a1_ref[...] * h_ref[...] + a2_ref[...] * hs_ref[...] + bu
    hn_ref[...] = hn
    y_ref[...] = jnp.einsum('gbp,gph->gbh', hn, cq_ref[...], precision=HIGHEST,
                            preferred_element_type=F32)


def _s5_step(u, h0_re, h0_im, ab_re, ab_im, bb_re, bb_im, c_re, c_im):
    b, d = u.shape
    g, p = ab_re.shape
    h = GROUP_SIZE
    u3 = u.reshape(b, g, h).transpose(1, 0, 2)
    hc = jnp.concatenate([h0_re, h0_im], axis=-1).transpose(1, 0, 2).astype(F32)
    hs = jnp.concatenate([h0_im, h0_re], axis=-1).transpose(1, 0, 2).astype(F32)
    a1 = jnp.concatenate([ab_re, ab_re], axis=-1)[:, None, :]
    a2 = jnp.concatenate([-ab_im, ab_im], axis=-1)[:, None, :]
    bbt = jnp.concatenate([bb_re, bb_im], axis=1).transpose(0, 2, 1)
    cq = jnp.concatenate([c_re.astype(F32), -c_im.astype(F32)], axis=-1).transpose(0, 2, 1)
    y3, hn = pl.pallas_call(
        _s5_step_kernel,
        out_shape=[jax.ShapeDtypeStruct((g, b, h), F32), jax.ShapeDtypeStruct((g, b, 2 * p), F32)],
        compiler_params=pltpu.CompilerParams(vmem_limit_bytes=V7X_VMEM_LIMIT),
        name="s5_step",
    )(u3, hc, hs, a1, a2, bbt, cq)
    y = y3.transpose(1, 0, 2).reshape(b, d)
    hn = hn.transpose(1, 0, 2)
    return y, hn[..., :p], hn[..., p:]


def _glu_kernel(x_ref, y_ref, g_ref, d_ref, sh_ref, sc_ref, gate_ref, w_ref, o_ref):
    x = x_ref[...]
    dm = x.shape[1]
    u = (_rms_hat(x) * g_ref[...]) * (1.0 + sc_ref[...]) + sh_ref[...]
    y = y_ref[...].astype(F32) + d_ref[...] * u
    z = jnp.dot(jax.nn.gelu(y).astype(BF16), w_ref[...], preferred_element_type=F32)
    out = z[:, :dm] * jax.nn.sigmoid(z[:, dm:])
    o_ref[...] = x + gate_ref[...] * out


def _glu(x, y_ssm, g, d_skip, sh, sc, gate, w_glu, tm):
    m, d = x.shape
    return pl.pallas_call(
        _glu_kernel,
        grid=(m // tm,),
        in_specs=[_row_spec(tm, d), _row_spec(tm, d), _const_spec((1, d)), _const_spec((1, d)),
                  _mod_spec(sh, tm), _mod_spec(sc, tm), _mod_spec(gate, tm), _const_spec((d, 2 * d))],
        out_specs=_row_spec(tm, d),
        out_shape=jax.ShapeDtypeStruct((m, d), F32),
        compiler_params=_cparams(("arbitrary",)),
        name="glu",
    )(x, y_ssm, g, d_skip, sh, sc, gate, w_glu)


def _mlp_kernel(x_ref, g_ref, sh_ref, sc_ref, gate_ref, w1_ref, w2_ref, o_ref, *, ff_chunk):
    x = x_ref[...]
    xn = ((_rms_hat(x) * g_ref[...]) * (1.0 + sc_ref[...]) + sh_ref[...]).astype(BF16)
    acc = jnp.zeros(x.shape, F32)
    for c in range(w1_ref.shape[1] // ff_chunk):
        a = jnp.dot(xn, w1_ref[:, c * ff_chunk:(c + 1) * ff_chunk], preferred_element_type=F32)
        a = jnp.square(jnp.maximum(a, 0.0)).astype(BF16)
        acc = acc + jnp.dot(a, w2_ref[c * ff_chunk:(c + 1) * ff_chunk, :], preferred_element_type=F32)
    o_ref[...] = x + gate_ref[...] * acc


def _mlp(x, g, sh, sc, gate, w1, w2, tm, ff_chunk=1024):
    m, d = x.shape
    dff = w1.shape[1]
    return pl.pallas_call(
        functools.partial(_mlp_kernel, ff_chunk=ff_chunk),
        grid=(m // tm,),
        in_specs=[_row_spec(tm, d), _const_spec((1, d)), _mod_spec(sh, tm), _mod_spec(sc, tm),
                  _mod_spec(gate, tm), _const_spec((d, dff)), _const_spec((dff, d))],
        out_specs=_row_spec(tm, d),
        out_shape=jax.ShapeDtypeStruct((m, d), F32),
        compiler_params=_cparams(("arbitrary",)),
        name="mlp",
    )(x, g, sh, sc, gate, w1, w2)


def _head_norm(t, gain, e_ref, et_ref):
    msq = jnp.dot((t * t).astype(BF16), e_ref[...], preferred_element_type=F32)
    r = lax.rsqrt(msq + EPS)
    r_hi = r.astype(BF16)
    r_lo = (r - r_hi.astype(F32)).astype(BF16)
    r_full = (jnp.dot(r_hi, et_ref[...], preferred_element_type=F32)
              + jnp.dot(r_lo, et_ref[...], preferred_element_type=F32))
    return (t * r_full) * gain


def _qkv_kernel(h_ref, gkv_ref, shkv_ref, sckv_ref, gq_ref, shq_ref, scq_ref,
                wkvf_ref, bf_ref, wq_ref, kn_ref, qn_ref, e_ref, et_ref,
                k_ref, v_ref, lf_ref, q_ref, *, q_scale):
    x = h_ref[...]
    dm = x.shape[1]
    xh = _rms_hat(x)
    xkv = ((xh * gkv_ref[...]) * (1.0 + sckv_ref[...]) + shkv_ref[...]).astype(BF16)
    xq = ((xh * gq_ref[...]) * (1.0 + scq_ref[...]) + shq_ref[...]).astype(BF16)
    proj = jnp.dot(xkv, wkvf_ref[...], preferred_element_type=F32)
    k_ref[...] = _head_norm(proj[:, :dm], kn_ref[...], e_ref, et_ref)
    v_ref[...] = proj[:, dm:2 * dm]
    lf = jax.nn.log_sigmoid(proj[:, 2 * dm:] + bf_ref[...])
    lf_ref[...] = lf[:, :lf_ref.shape[1]]
    q = _head_norm(jnp.dot(xq, wq_ref[...], preferred_element_type=F32), qn_ref[...], e_ref, et_ref)
    q_ref[...] = (q * q_scale).astype(q_ref.dtype)


def _qkv(h, g_kv, sh_kv, sc_kv, g_q, sh_q, sc_q, w_kvf, b_f, w_q, k_norm, q_norm, e, et, tm,
         q_scale):
    m, d = h.shape
    nkvf = w_kvf.shape[1]
    return pl.pallas_call(
        functools.partial(_qkv_kernel, q_scale=q_scale),
        grid=(m // tm,),
        in_specs=[_row_spec(tm, d),
                  _const_spec((1, d)), _mod_spec(sh_kv, tm), _mod_spec(sc_kv, tm),
                  _const_spec((1, d)), _mod_spec(sh_q, tm), _mod_spec(sc_q, tm),
                  _const_spec((d, nkvf)), _const_spec((1, V7X_LANES)), _const_spec((d, d)),
                  _const_spec((1, d)), _const_spec((1, d)),
                  _const_spec((d, V7X_LANES)), _const_spec((V7X_LANES, d))],
        out_specs=[_row_spec(tm, d), _row_spec(tm, d), _row_spec(tm, N_HEADS), _row_spec(tm, d)],
        out_shape=[jax.ShapeDtypeStruct((m, d), F32), jax.ShapeDtypeStruct((m, d), F32),
                   jax.ShapeDtypeStruct((m, N_HEADS), F32), jax.ShapeDtypeStruct((m, d), BF16)],
        compiler_params=_cparams(("arbitrary",)),
        name="qkv",
    )(h, g_kv, sh_kv, sc_kv, g_q, sh_q, sc_q, w_kvf, b_f, w_q, k_norm, q_norm, e, et)


def _cumsum_kernel(x_ref, o_ref, carry_ref):
    @pl.when(pl.program_id(0) == 0)
    def _():
        carry_ref[...] = jnp.zeros_like(carry_ref)
    tm = x_ref.shape[0]
    tri = (lax.broadcasted_iota(jnp.int32, (tm, tm), 1)
           <= lax.broadcasted_iota(jnp.int32, (tm, tm), 0)).astype(F32)
    f = jnp.dot(tri, x_ref[...], precision=HIGHEST, preferred_element_type=F32) + carry_ref[...]
    o_ref[...] = f
    carry_ref[...] = f[tm - 1:tm, :]


def _cumsum_rows(x, tm):
    m, n = x.shape
    return pl.pallas_call(
        _cumsum_kernel,
        grid=(m // tm,),
        in_specs=[_row_spec(tm, n)],
        out_specs=_row_spec(tm, n),
        out_shape=jax.ShapeDtypeStruct((m, n), F32),
        scratch_shapes=[pltpu.VMEM((1, n), F32)],
        compiler_params=_cparams(("arbitrary",)),
        name="logf_cumsum",
    )(x)


LOG2E = 1.4426950408889634
ATTN_AUG = 3
ATTN_VROWS = 80


def _attn_kernel(q_ref, k_ref, v_ref, o_ref, sa_scr, sb_scr, m_scr, acc_scr, *, hd):
    i = pl.program_id(1)
    q_t = q_ref[...]
    m_scr[...] = jnp.full(m_scr.shape, -jnp.inf, F32)
    acc_scr[...] = jnp.zeros(acc_scr.shape, F32)

    def scores(j, dst):
        dst[...] = jnp.dot(k_ref[j], q_t, preferred_element_type=F32)

    def consume(j, src, masked):
        s_t = src[...]
        if masked:
            key = lax.broadcasted_iota(jnp.int32, s_t.shape, 0)
            qry = lax.broadcasted_iota(jnp.int32, s_t.shape, 1)
            s_t = jnp.where(key <= qry, s_t, -jnp.inf)
        m_old = m_scr[...]
        m_new = jnp.maximum(m_old, jnp.max(s_t, axis=0, keepdims=True))
        alpha = jnp.exp2(m_old - m_new)
        p_t = jnp.exp2(s_t - m_new).astype(BF16)
        acc_scr[...] = alpha * acc_scr[...] + jnp.dot(v_ref[j], p_t, preferred_element_type=F32)
        m_scr[...] = m_new

    scores(0, sa_scr)

    def pair(jj, c):
        t0 = 2 * jj
        scores(t0 + 1, sb_scr)
        consume(t0, sa_scr, False)
        scores(t0 + 2, sa_scr)
        consume(t0 + 1, sb_scr, False)
        return c

    lax.fori_loop(0, i // 2, pair, 0)

    @pl.when(i % 2 == 0)
    def _():
        consume(i, sa_scr, True)

    @pl.when(i % 2 == 1)
    def _():
        scores(i, sb_scr)
        consume(i - 1, sa_scr, False)
        consume(i, sb_scr, True)

    acc = acc_scr[...]
    o_ref[...] = (acc[:hd] / acc[hd:hd + 1]).astype(o_ref.dtype)


def _trunc_bf16(x):
    bits = lax.bitcast_convert_type(x, jnp.uint32) & jnp.uint32(0xFFFF0000)
    return lax.bitcast_convert_type(bits, F32)


def _split3(x):
    hi = _trunc_bf16(x)
    r = x - hi
    mid = _trunc_bf16(r)
    lo = _trunc_bf16(r - mid)
    return hi.astype(BF16), mid.astype(BF16), lo.astype(BF16)


def _attn_prompt(q_bf16, k, v, fcum, tq):
    l, d = k.shape
    nh = N_HEADS
    hd = d // nh
    nt = l // tq
    q_t = q_bf16.reshape(l, nh, hd).transpose(1, 2, 0)
    q_aug = jnp.concatenate([q_t, jnp.ones((nh, ATTN_AUG, l), BF16),
                             jnp.zeros((nh, V7X_LANES - hd - ATTN_AUG, l), BF16)], axis=1)
    kh = k.astype(BF16).reshape(l, nh, hd).transpose(1, 0, 2)
    f3 = jnp.stack(_split3(-fcum.T * LOG2E), axis=-1)
    k_aug = jnp.concatenate([kh, f3, jnp.zeros((nh, l, V7X_LANES - hd - ATTN_AUG), BF16)], axis=-1)
    k_aug = k_aug.reshape(nh, nt, tq, V7X_LANES)
    v_t = v.astype(BF16).reshape(l, nh, hd).transpose(1, 2, 0)
    v_aug = jnp.concatenate([v_t, jnp.ones((nh, 1, l), BF16),
                             jnp.zeros((nh, ATTN_VROWS - hd - 1, l), BF16)], axis=1)
    v_aug = v_aug.reshape(nh, ATTN_VROWS, nt, tq).transpose(0, 2, 1, 3)
    o_t = pl.pallas_call(
        functools.partial(_attn_kernel, hd=hd),
        grid=(nh, nt),
        in_specs=[pl.BlockSpec((None, V7X_LANES, tq), lambda h, i: (h, 0, i)),
                  pl.BlockSpec((None, nt, tq, V7X_LANES), lambda h, i: (h, 0, 0, 0)),
                  pl.BlockSpec((None, nt, ATTN_VROWS, tq), lambda h, i: (h, 0, 0, 0))],
        out_specs=pl.BlockSpec((None, hd, tq), lambda h, i: (h, 0, i)),
        out_shape=jax.ShapeDtypeStruct((nh, hd, l), BF16),
        scratch_shapes=[pltpu.VMEM((tq, tq), F32), pltpu.VMEM((tq, tq), F32),
                        pltpu.VMEM((1, tq), F32), pltpu.VMEM((ATTN_VROWS, tq), F32)],
        compiler_params=_cparams(("arbitrary", "arbitrary")),
        name="attn_prompt",
    )(q_aug, k_aug, v_aug)
    return o_t.transpose(2, 0, 1).reshape(l, d)


def _attn_paged_kernel(pt_ref, qb_ref, knew_ref, vnew_ref, lfnew_ref, *refs, pages_per_step, hd):
    del pt_ref
    pps = pages_per_step
    k_refs = refs[:pps]
    v_refs = refs[pps:2 * pps]
    lf_refs = refs[2 * pps:3 * pps]
    o_ref, m_scr, l_scr, f_scr, s_scr, p_scr, a_scr, acc_scr = refs[3 * pps:]
    nh = N_HEADS
    d, ps = acc_scr.shape
    step = pl.program_id(1)

    @pl.when(step == 0)
    def _():
        m_scr[...] = jnp.full(m_scr.shape, -jnp.inf, F32)
        l_scr[...] = jnp.zeros(l_scr.shape, F32)
        f_scr[...] = jnp.zeros(f_scr.shape, F32)
        acc_scr[...] = jnp.zeros(acc_scr.shape, F32)

    def page(k_rows, v_rows, f, n_valid):
        for h in range(nh):
            rows = slice(h * hd, (h + 1) * hd)
            s_scr[h:h + 1, :] = jnp.sum(k_rows(rows) * qb_ref[rows, :], axis=0, keepdims=True)
        z = s_scr[...] - f
        if n_valid is not None:
            z = jnp.where(lax.broadcasted_iota(jnp.int32, z.shape, 1) < n_valid, z, -jnp.inf)
        m_old = m_scr[...]
        m_new = jnp.maximum(m_old, jnp.max(z, axis=1, keepdims=True))
        alpha = jnp.exp(m_old - m_new)
        p = jnp.exp(z - m_new)
        l_scr[...] = alpha * l_scr[...] + jnp.sum(p, axis=1, keepdims=True)
        m_scr[...] = m_new
        p_scr[...] = p
        a_scr[...] = jnp.broadcast_to(alpha, (nh, ps))
        for h in range(nh):
            rows = slice(h * hd, (h + 1) * hd)
            p_h = jnp.broadcast_to(p_scr[h:h + 1, :], (hd, ps))
            a_h = jnp.broadcast_to(a_scr[h:h + 1, :], (hd, ps))
            acc_scr[rows, :] = a_h * acc_scr[rows, :] + p_h * v_rows(rows)

    tri = (lax.broadcasted_iota(jnp.int32, (ps, ps), 0)
           <= lax.broadcasted_iota(jnp.int32, (ps, ps), 1)).astype(F32)
    for r in range(pps):
        f = jnp.dot(lf_refs[r][...], tri, precision=HIGHEST, preferred_element_type=F32) + f_scr[...]
        page(lambda rows, r=r: k_refs[r][rows, :], lambda rows, r=r: v_refs[r][rows, :], f, None)
        f_scr[...] = f[:, ps - 1:ps]

    @pl.when(step == pl.num_programs(1) - 1)
    def _():
        f_new = jnp.broadcast_to(f_scr[...] + lfnew_ref[...], (nh, ps))
        page(lambda rows: jnp.broadcast_to(knew_ref[rows, :], (hd, ps)),
             lambda rows: jnp.broadcast_to(vnew_ref[rows, :], (hd, ps)), f_new, 1)
        for h in range(nh):
            rows = slice(h * hd, (h + 1) * hd)
            o_h = jnp.sum(acc_scr[rows, :], axis=1, keepdims=True)
            o_ref[rows, :] = o_h / l_scr[h:h + 1, :]


def _attn_paged(q_bf16, k_new, v_new, lf_new, cache_k, cache_v, cache_logf, page_table,
                pages_per_step=8):
    b, d = k_new.shape
    nh = N_HEADS
    hd = d // nh
    n_pages = page_table.shape[1]
    n_phys, ps = cache_k.shape[0], cache_k.shape[1]
    pps = math.gcd(pages_per_step, n_pages)
    ck = cache_k.transpose(0, 2, 3, 1).reshape(n_phys, d, ps)
    cv = cache_v.transpose(0, 2, 3, 1).reshape(n_phys, d, ps)
    clf = cache_logf.transpose(0, 2, 1).astype(F32)
    qb = jnp.broadcast_to(q_bf16.astype(F32)[:, :, None], (b, d, ps))
    pt = page_table.reshape(-1).astype(jnp.int32)

    def page_map(r):
        return lambda s, g, pt_ref: (pt_ref[s * n_pages + g * pps + r], 0, 0)

    seq = lambda rows, width: pl.BlockSpec((None, rows, width), lambda s, g, pt_ref: (s, 0, 0))
    in_specs = ([seq(d, ps), seq(d, 1), seq(d, 1), seq(nh, 1)]
                + [pl.BlockSpec((None, d, ps), page_map(r)) for r in range(pps)]
                + [pl.BlockSpec((None, d, ps), page_map(r)) for r in range(pps)]
                + [pl.BlockSpec((None, nh, ps), page_map(r)) for r in range(pps)])
    out = pl.pallas_call(
        functools.partial(_attn_paged_kernel, pages_per_step=pps, hd=hd),
        grid_spec=pltpu.PrefetchScalarGridSpec(
            num_scalar_prefetch=1,
            grid=(b, n_pages // pps),
            in_specs=in_specs,
            out_specs=seq(d, 1),
            scratch_shapes=[pltpu.VMEM((nh, 1), F32), pltpu.VMEM((nh, 1), F32), pltpu.VMEM((nh, 1), F32),
                            pltpu.VMEM((nh, ps), F32), pltpu.VMEM((nh, ps), F32),
                            pltpu.VMEM((nh, ps), F32), pltpu.VMEM((d, ps), F32)]),
        out_shape=jax.ShapeDtypeStruct((b, d, 1), F32),
        compiler_params=_cparams(("arbitrary", "arbitrary")),
        name="attn_paged",
    )(pt, qb, k_new.reshape(b, d, 1), v_new.reshape(b, d, 1), lf_new.reshape(b, nh, 1),
      *([ck] * pps), *([cv] * pps), *([clf] * pps))
    return out.reshape(b, d).astype(BF16)


def _oproj_kernel(o_ref, h_ref, gate_ref, w_ref, out_ref):
    out_ref[...] = h_ref[...] + gate_ref[...] * jnp.dot(o_ref[...], w_ref[...],
                                                        preferred_element_type=F32)


def _oproj(o, h, gate, w_o, tm):
    m, d = h.shape
    return pl.pallas_call(
        _oproj_kernel,
        grid=(m // tm,),
        in_specs=[_row_spec(tm, d), _row_spec(tm, d), _mod_spec(gate, tm), _const_spec((d, d))],
        out_specs=_row_spec(tm, d),
        out_shape=jax.ShapeDtypeStruct((m, d), F32),
        compiler_params=_cparams(("arbitrary",)),
        name="oproj",
    )(o, h, gate, w_o)


def _row_tile(m):
    return min(m, 512)


def _layer_mods(mod_all, kv_all, rows, d):
    per_layer = [[mod_all[layer, rows, i * d:(i + 1) * d] for i in range(6)]
                 for layer in range(mod_all.shape[0])]
    kv = [kv_all[0, rows, i * d:(i + 1) * d] for i in range(2)]
    return per_layer, kv


def kernel(x_prompt, x_sample, c_prompt, c_sample, state_ssm_re, state_ssm_im, cache_k, cache_v, cache_logf, page_table, ada_w, ada_b, g_mix, g_mlp, mlp_w1, mlp_w2, ssm_lam_re, ssm_lam_im, ssm_log_dt, ssm_b_re, ssm_b_im, ssm_c_re, ssm_c_im, ssm_d, w_glu, kv_ada_w, kv_ada_b, g_kv, w_kvf, b_f, k_norm, w_q, q_norm, w_o):
    bp, l, d = x_prompt.shape
    bs = x_sample.shape[0]
    nh = N_HEADS
    hd = d // nh
    assert bp == 1 and x_sample.shape[1] == 1 and ada_w.shape[0] == 2
    assert l % (S5_CHUNK * S5_SEGMENTS * 8) == 0

    row = lambda a: a.reshape(1, -1).astype(F32)
    w_glu_b = w_glu[0].astype(BF16)
    w1_b = mlp_w1.astype(BF16)
    w2_b = mlp_w2.astype(BF16)
    w_q_b = w_q[0].astype(BF16)
    w_o_b = w_o[0].astype(BF16)
    w_kvf_b = jnp.concatenate(
        [w_kvf, jnp.zeros((d, V7X_LANES - nh), w_kvf.dtype)], axis=1).astype(BF16)
    b_f_p = jnp.concatenate([b_f.astype(F32), jnp.zeros((V7X_LANES - nh,), F32)]).reshape(1, -1)
    head_of = jnp.arange(d) // hd
    e = (head_of[:, None] == jnp.arange(V7X_LANES)[None, :]).astype(F32)
    e_mean = (e / hd).astype(BF16)
    e_t = e.T.astype(BF16)
    k_norm_full = jnp.tile(k_norm.astype(F32), nh).reshape(1, d)
    q_norm_full = jnp.tile(q_norm[0].astype(F32), nh).reshape(1, d)

    ab_re, ab_im, bb_re, bb_im = _s5_discretise(ssm_lam_re[0], ssm_lam_im[0], ssm_log_dt[0],
                                                ssm_b_re[0], ssm_b_im[0])
    n_local = l // S5_CHUNK // S5_SEGMENTS
    s5_w = _s5_chunk_weights(ab_re, ab_im, bb_re, bb_im, ssm_c_re[0], ssm_c_im[0], n_local)

    n_c = bp + bs
    c_rows = -(-n_c // 8) * 8
    c_all = jnp.concatenate([c_prompt, c_sample, jnp.zeros((c_rows - n_c, d), F32)], axis=0)
    mod_all = _mods(c_all, ada_w, ada_b.reshape(ada_b.shape[0], 1, -1))
    kv_all = _mods(c_all, kv_ada_w[None], kv_ada_b.reshape(1, 1, -1))

    def trunk(x, rows, s5_fn, attend, q_scale):
        m = x.shape[0]
        tm = _row_tile(m)
        (l0, l1), (sh_kv, sc_kv) = _layer_mods(mod_all, kv_all, rows, d)
        sh_m, sc_m, g_m, sh_f, sc_f, g_f = l0
        y_ssm, st_re, st_im = s5_fn(x, row(g_mix[0]), sh_m, sc_m, tm)
        h = _glu(x, y_ssm, row(g_mix[0]), row(ssm_d[0]), sh_m, sc_m, g_m, w_glu_b, tm)
        h = _mlp(h, row(g_mlp[0]), sh_f, sc_f, g_f, w1_b[0], w2_b[0], tm)
        sh_m, sc_m, g_m, sh_f, sc_f, g_f = l1
        k, v, lf, q = _qkv(h, row(g_kv), sh_kv, sc_kv, row(g_mix[1]), sh_m, sc_m,
                           w_kvf_b, b_f_p, w_q_b, k_norm_full, q_norm_full, e_mean, e_t, tm, q_scale)
        o = attend(q, k, v, lf)
        h = _oproj(o, h, g_m, w_o_b, tm)
        h = _mlp(h, row(g_mlp[1]), sh_f, sc_f, g_f, w1_b[1], w2_b[1], tm)
        return h, st_re, st_im, k, v, lf

    def s5_prompt(x, g, sh, sc, tm):
        u = _ada_rms(x, g, sh, sc, tm, BF16)
        return _s5_prompt(u, s5_w)

    def attend_prompt(q, k, v, lf):
        fcum = _cumsum_rows(lf, _row_tile(l))
        return _attn_prompt(q, k, v, fcum, min(l, 512))

    y_p, sre_p, sim_p, k_p, v_p, f_p = trunk(x_prompt[0], slice(0, 1), s5_prompt, attend_prompt,
                                             LOG2E / math.sqrt(hd))

    def s5_sample(x, g, sh, sc, tm):
        u = _ada_rms(x, g, sh, sc, tm, F32)
        return _s5_step(u, state_ssm_re[0], state_ssm_im[0], ab_re, ab_im, bb_re, bb_im,
                        ssm_c_re[0], ssm_c_im[0])

    def attend_sample(q, k, v, lf):
        return _attn_paged(q, k, v, lf, cache_k, cache_v, cache_logf, page_table)

    y_s, sre_s, sim_s, k_s, v_s, f_s = trunk(x_sample[:, 0], slice(bp, bp + bs), s5_sample,
                                             attend_sample, 1.0 / math.sqrt(hd))

    sdt = state_ssm_re.dtype
    return (y_p[None], y_s[:, None],
            sre_p[None, None].astype(sdt), sim_p[None, None].astype(sdt),
            k_p.reshape(1, l, nh, hd), v_p.reshape(1, l, nh, hd), f_p[None],
            sre_s[None].astype(sdt), sim_s[None].astype(sdt),
            k_s.reshape(bs, 1, nh, hd), v_s.reshape(bs, 1, nh, hd), f_s[:, None])
```

```python
import functools
import math

import jax
import jax.numpy as jnp
from jax import lax
from jax.experimental import pallas as pl
from jax.experimental.pallas import tpu as pltpu

F32 = jnp.float32
BF16 = jnp.bfloat16
HIGHEST = lax.Precision.HIGHEST

EPS = 1e-6
GROUP_SIZE = 16
STATE_DIM = 64
N_HEADS = 16
PAGE_SIZE = 128
S5_CHUNK = 8
S5_SEGMENTS = 8
S5_ROWS_PER_SEG = 64
V7X_LANES = 128
V7X_VMEM_LIMIT = 56 * 1024 * 1024


def _cparams(sem):
    return pltpu.CompilerParams(dimension_semantics=sem, vmem_limit_bytes=V7X_VMEM_LIMIT)


def _const_spec(shape):
    nd = len(shape)
    return pl.BlockSpec(shape, lambda *_: (0,) * nd, pipeline_mode=pl.Buffered(1))


def _row_spec(tm, width):
    return pl.BlockSpec((tm, width), lambda i: (i, 0))


def _mod_spec(arr, tm):
    if arr.shape[0] == 1:
        return pl.BlockSpec((1, arr.shape[1]), lambda i: (0, 0))
    return pl.BlockSpec((tm, arr.shape[1]), lambda i: (i, 0))


def _rms_hat(x):
    return x * lax.rsqrt(jnp.mean(x * x, axis=-1, keepdims=True) + EPS)


def _mods_kernel(c_ref, w_ref, b_ref, o_ref):
    o_ref[...] = jnp.dot(c_ref[...], w_ref[...], precision=HIGHEST,
                         preferred_element_type=F32) + b_ref[...]


def _mods(c, w3, b3, tn=2048):
    nl, d, n = w3.shape
    mp = c.shape[0]
    return pl.pallas_call(
        _mods_kernel,
        grid=(nl, n // tn),
        in_specs=[pl.BlockSpec((mp, d), lambda l, j: (0, 0)),
                  pl.BlockSpec((None, d, tn), lambda l, j: (l, 0, j)),
                  pl.BlockSpec((None, 1, tn), lambda l, j: (l, 0, j))],
        out_specs=pl.BlockSpec((None, mp, tn), lambda l, j: (l, 0, j)),
        out_shape=jax.ShapeDtypeStruct((nl, mp, n), F32),
        compiler_params=_cparams(("arbitrary", "arbitrary")),
        name="mods",
    )(c, w3, b3)


def _ada_rms_kernel(x_ref, g_ref, sh_ref, sc_ref, o_ref):
    u = (_rms_hat(x_ref[...]) * g_ref[...]) * (1.0 + sc_ref[...]) + sh_ref[...]
    o_ref[...] = u.astype(o_ref.dtype)


def _ada_rms(x, g, sh, sc, tm, out_dtype):
    m, d = x.shape
    return pl.pallas_call(
        _ada_rms_kernel,
        grid=(m // tm,),
        in_specs=[_row_spec(tm, d), _const_spec((1, d)), _mod_spec(sh, tm), _mod_spec(sc, tm)],
        out_specs=_row_spec(tm, d),
        out_shape=jax.ShapeDtypeStruct((m, d), out_dtype),
        compiler_params=_cparams(("arbitrary",)),
        name="ada_rms",
    )(x, g, sh, sc)


def _chunk_rows(j, c, rows_per_seg):
    return pl.ds(c * S5_CHUNK + j, S5_SEGMENTS, stride=rows_per_seg)


def _ada_rms_chunked_kernel(x_ref, g_ref, sh_ref, sc_ref, o_ref, u_scr):
    seg, rps, d = x_ref.shape
    x = x_ref[...].reshape(seg * rps, d)
    u = (_rms_hat(x) * g_ref[...]) * (1.0 + sc_ref[...]) + sh_ref[...]
    for k in range(d // V7X_LANES):
        u_scr[k] = u[:, k * V7X_LANES:(k + 1) * V7X_LANES]
        for j in range(S5_CHUNK):
            cols = slice(j * V7X_LANES, (j + 1) * V7X_LANES)
            for c in range(0, rps // S5_CHUNK, 2):
                pair = jnp.concatenate([u_scr[k, _chunk_rows(j, c, rps), :],
                                        u_scr[k, _chunk_rows(j, c + 1, rps), :]], axis=0)
                o_ref[k, c * seg:(c + 2) * seg, cols] = pair.astype(o_ref.dtype)


def _ada_rms_chunked(x, g, sh, sc):
    l, d = x.shape
    seg, rps, t = S5_SEGMENTS, S5_ROWS_PER_SEG, S5_CHUNK
    nb = d // V7X_LANES
    return pl.pallas_call(
        _ada_rms_chunked_kernel,
        grid=(l // (seg * rps),),
        in_specs=[pl.BlockSpec((seg, rps, d), lambda i: (0, i, 0)), _const_spec((1, d)),
                  _const_spec((1, d)), _const_spec((1, d))],
        out_specs=pl.BlockSpec((nb, seg * rps // t, t * V7X_LANES), lambda i: (0, i, 0)),
        out_shape=jax.ShapeDtypeStruct((nb, l // t, t * V7X_LANES), BF16),
        scratch_shapes=[pltpu.VMEM((nb, seg * rps, V7X_LANES), F32)],
        compiler_params=_cparams(("arbitrary",)),
        name="ada_rms_chunked",
    )(x.reshape(seg, l // seg, d), g, sh, sc)


def _s5_discretise(lam_re, lam_im, log_dt, b_re, b_im):
    dt = jnp.exp(log_dt.astype(F32))[:, None]
    lr = lam_re.astype(F32)
    li = lam_im.astype(F32)
    mag = jnp.exp(lr * dt)
    ang = li * dt
    ab_re = mag * jnp.cos(ang)
    ab_im = mag * jnp.sin(ang)
    den = lr * lr + li * li
    zr = ((ab_re - 1) * lr + ab_im * li) / den
    zi = (ab_im * lr - (ab_re - 1) * li) / den
    br = b_re.astype(F32)
    bi = b_im.astype(F32)
    bb_re = zr[..., None] * br - zi[..., None] * bi
    bb_im = zr[..., None] * bi + zi[..., None] * br
    return ab_re, ab_im, bb_re, bb_im


def _cpow_table(ar, ai, n):
    tr = jnp.ones((1,) + ar.shape, F32)
    ti = jnp.zeros((1,) + ar.shape, F32)
    br, bi = ar, ai
    while tr.shape[0] < n:
        tr, ti = (jnp.concatenate([tr, tr * br - ti * bi], axis=0),
                  jnp.concatenate([ti, tr * bi + ti * br], axis=0))
        br, bi = br * br - bi * bi, 2.0 * br * bi
    return tr[:n], ti[:n]


def _block_lanes(x_re, x_im, nb):
    def blk(x):
        x = x.reshape((nb, x.shape[0] // nb) + x.shape[1:])
        x = jnp.moveaxis(x, 1, -2)
        return x.reshape(x.shape[:-2] + (x.shape[-2] * x.shape[-1],))
    return jnp.concatenate([blk(x_re), blk(x_im)], axis=-1)


_ein = functools.partial(jnp.einsum, precision=HIGHEST)


def _expand_groups(compact, row_w, col_w, gb):
    nb, r, c = compact.shape
    src = jnp.arange(c)
    dst = jnp.arange(c * gb)
    dst_src = (dst // (col_w * gb)) * col_w + dst % col_w
    rep = (src[:, None] == dst_src[None, :]).astype(BF16)
    wide = jnp.einsum('krc,cd->krd', compact.astype(BF16), rep, preferred_element_type=BF16)
    row_group = (jnp.arange(r) // row_w) % gb
    col_group = (dst // col_w) % gb
    return jnp.where(row_group[:, None] == col_group[None, :], wide, jnp.zeros((), BF16))


def _s5_chunk_weights(ab_re, ab_im, bb_re, bb_im, c_re, c_im, n_local):
    t = S5_CHUNK
    g, p = ab_re.shape
    h = GROUP_SIZE
    gb = V7X_LANES // h
    nb = g // gb
    cr = c_re.astype(F32)
    ci = c_im.astype(F32)
    pw_re, pw_im = _cpow_table(ab_re, ab_im, t + 1)
    cb_rr = _ein('gop,gph->gpoh', cr, bb_re) - _ein('gop,gph->gpoh', ci, bb_im)
    cb_ii = _ein('gop,gph->gpoh', cr, bb_im) + _ein('gop,gph->gpoh', ci, bb_re)
    kern = _ein('tgp,gpoh->gtoh', pw_re[:t], cb_rr) - _ein('tgp,gpoh->gtoh', pw_im[:t], cb_ii)
    tau = jnp.arange(t)[None, :] - jnp.arange(t)[:, None]
    m = jnp.where((tau >= 0)[None, :, :, None, None], kern[:, jnp.clip(tau, 0, t - 1)], 0.0)
    m = m.reshape(nb, gb, t, t, h, h)
    m_all = _expand_groups(m.transpose(0, 2, 1, 5, 3, 4).reshape(nb, t * gb * h, t * h), h, h, gb)
    rev = t - 1 - jnp.arange(t)
    p_re = _ein('jgp,gph->gjhp', pw_re[rev], bb_re) - _ein('jgp,gph->gjhp', pw_im[rev], bb_im)
    p_im = _ein('jgp,gph->gjhp', pw_re[rev], bb_im) + _ein('jgp,gph->gjhp', pw_im[rev], bb_re)
    p_ri = jnp.stack([p_re, p_im], axis=1).reshape(nb, gb, 2, t, h, p)
    p_all = _expand_groups(p_ri.transpose(0, 3, 1, 4, 2, 5).reshape(nb, t * gb * h, 2 * p), h, p, gb)
    w_re = _ein('gop,tgp->gpto', cr, pw_re[1:]) - _ein('gop,tgp->gpto', ci, pw_im[1:])
    w_im = _ein('gop,tgp->gpto', cr, pw_im[1:]) + _ein('gop,tgp->gpto', ci, pw_re[1:])
    w_ri = jnp.stack([w_re, -w_im], axis=1).reshape(nb, gb, 2, p, t, h)
    q_all = _expand_groups(w_ri.transpose(0, 2, 1, 3, 4, 5).reshape(nb, 2 * gb * p, t * h), p, h, gb)
    at_re, at_im = pw_re[t], pw_im[t]
    lp_re, lp_im = _cpow_table(at_re, at_im, n_local + 1)
    apow = _block_lanes(jnp.moveaxis(lp_re[:n_local], 0, 1), jnp.moveaxis(lp_im[:n_local], 0, 1), nb)
    a_chunk = _block_lanes(at_re[:, None], at_im[:, None], nb)
    a_seg = _block_lanes(lp_re[n_local][:, None], lp_im[n_local][:, None], nb)
    return m_all.astype(BF16), p_all.astype(BF16), q_all.astype(BF16), a_chunk, apow, a_seg


def _s5_prompt_kernel(u_ref, p_ref, m_ref, q_ref, ac_ref, apow_ref, aseg_ref,
                      y_ref, fin_ref, sp_scr, *, n_local):
    seg = S5_SEGMENTS
    half = sp_scr.shape[1] // 2
    sp_scr[...] = jnp.dot(u_ref[...], p_ref[...], preferred_element_type=F32)
    ar = jnp.broadcast_to(ac_ref[:, :half], (seg, half))
    ai = jnp.broadcast_to(ac_ref[:, half:], (seg, half))

    def scan_step(k, carry):
        sr, si = carry
        rows = pl.ds(pl.multiple_of(k * seg, seg), seg)
        gr = sp_scr[rows, :half]
        gi = sp_scr[rows, half:]
        sp_scr[rows, :half] = sr
        sp_scr[rows, half:] = si
        return (ar * sr - ai * si + gr, ar * si + ai * sr + gi)

    zero = jnp.zeros((seg, half), F32)
    er, ei = lax.fori_loop(0, n_local, scan_step, (zero, zero))

    sgr = aseg_ref[:, :half]
    sgi = aseg_ref[:, half:]
    cr = jnp.zeros((1, half), F32)
    ci = jnp.zeros((1, half), F32)
    rows_r, rows_i = [], []
    for s in range(seg):
        rows_r.append(cr)
        rows_i.append(ci)
        cr, ci = (sgr * cr - sgi * ci + er[s:s + 1], sgr * ci + sgi * cr + ei[s:s + 1])
    fin_ref[...] = jnp.concatenate([cr, ci], axis=-1)
    sin_r = jnp.concatenate(rows_r, axis=0)
    sin_i = jnp.concatenate(rows_i, axis=0)

    def fix_step(k, c):
        rows = pl.ds(pl.multiple_of(k * seg, seg), seg)
        apk = apow_ref[pl.ds(k, 1), :]
        pr = jnp.broadcast_to(apk[:, :half], (seg, half))
        pi = jnp.broadcast_to(apk[:, half:], (seg, half))
        sp_scr[rows, :half] += pr * sin_r - pi * sin_i
        sp_scr[rows, half:] += pr * sin_i + pi * sin_r
        return c

    lax.fori_loop(0, n_local, fix_step, 0)
    y = (jnp.dot(sp_scr[...].astype(BF16), q_ref[...], preferred_element_type=F32)
         + jnp.dot(u_ref[...], m_ref[...], preferred_element_type=F32))
    y_ref[...] = y.astype(y_ref.dtype)


def _s5_prompt(u_chunked, weights):
    m_all, p_all, q_all, a_chunk, apow, a_seg = weights
    nb, nc, kw = u_chunked.shape
    n_local = nc // S5_SEGMENTS
    sw = p_all.shape[2]
    blk = lambda rows, cols: pl.BlockSpec((None, rows, cols), lambda k: (k, 0, 0))
    yt, fin = pl.pallas_call(
        functools.partial(_s5_prompt_kernel, n_local=n_local),
        grid=(nb,),
        in_specs=[blk(nc, kw), blk(kw, sw), blk(kw, kw), blk(sw, kw), blk(1, sw), blk(n_local, sw),
                  blk(1, sw)],
        out_specs=[blk(nc, kw), blk(1, sw)],
        out_shape=[jax.ShapeDtypeStruct((nb, nc, kw), BF16), jax.ShapeDtypeStruct((nb, 1, sw), F32)],
        scratch_shapes=[pltpu.VMEM((nc, sw), F32)],
        compiler_params=_cparams(("arbitrary",)),
        name="s5_prompt",
    )(u_chunked, p_all, m_all, q_all, a_chunk, apow, a_seg)
    fin = fin.reshape(nb, 2, -1, STATE_DIM)
    return yt, fin[:, 0].reshape(-1, STATE_DIM), fin[:, 1].reshape(-1, STATE_DIM)


def _s5_step_kernel(u_ref, h_ref, hs_ref, a1_ref, a2_ref, bbt_ref, cq_ref, y_ref, hn_ref):
    bu = jnp.einsum('gbh,ghp->gbp', u_ref[...], bbt_ref[...], precision=HIGHEST,
                    preferred_element_type=F32)
    hn = a1_ref[...] * h_ref[...] + a2_ref[...] * hs_ref[...] + bu
    hn_ref[...] = hn
    y_ref[...] = jnp.einsum('gbp,gph->gbh', hn, cq_ref[...], precision=HIGHEST,
                            preferred_element_type=F32)


def _s5_step(u, h0_re, h0_im, ab_re, ab_im, bb_re, bb_im, c_re, c_im):
    b, d = u.shape
    g, p = ab_re.shape
    h = GROUP_SIZE
    u3 = u.reshape(b, g, h).transpose(1, 0, 2)
    hc = jnp.concatenate([h0_re, h0_im], axis=-1).transpose(1, 0, 2).astype(F32)
    hs = jnp.concatenate([h0_im, h0_re], axis=-1).transpose(1, 0, 2).astype(F32)
    a1 = jnp.concatenate([ab_re, ab_re], axis=-1)[:, None, :]
    a2 = jnp.concatenate([-ab_im, ab_im], axis=-1)[:, None, :]
    bbt = jnp.concatenate([bb_re, bb_im], axis=1).transpose(0, 2, 1)
    cq = jnp.concatenate([c_re.astype(F32), -c_im.astype(F32)], axis=-1).transpose(0, 2, 1)
    y3, hn = pl.pallas_call(
        _s5_step_kernel,
        out_shape=[jax.ShapeDtypeStruct((g, b, h), F32), jax.ShapeDtypeStruct((g, b, 2 * p), F32)],
        compiler_params=pltpu.CompilerParams(vmem_limit_bytes=V7X_VMEM_LIMIT),
        name="s5_step",
    )(u3, hc, hs, a1, a2, bbt, cq)
    y = y3.transpose(1, 0, 2).reshape(b, d)
    hn = hn.transpose(1, 0, 2)
    return y, hn[..., :p], hn[..., p:]


def _glu_body(x, y_ssm, g_ref, d_ref, sh_ref, sc_ref, gate_ref, w_ref):
    dm = x.shape[1]
    u = (_rms_hat(x) * g_ref[...]) * (1.0 + sc_ref[...]) + sh_ref[...]
    y = y_ssm + d_ref[...] * u
    z = jnp.dot(jax.nn.gelu(y).astype(BF16), w_ref[...], preferred_element_type=F32)
    out = z[:, :dm] * jax.nn.sigmoid(z[:, dm:])
    return x + gate_ref[...] * out


def _glu_kernel(x_ref, y_ref, g_ref, d_ref, sh_ref, sc_ref, gate_ref, w_ref, o_ref):
    o_ref[...] = _glu_body(x_ref[...], y_ref[...].astype(F32), g_ref, d_ref, sh_ref, sc_ref,
                           gate_ref, w_ref)


def _glu(x, y_ssm, g, d_skip, sh, sc, gate, w_glu, tm):
    m, d = x.shape
    return pl.pallas_call(
        _glu_kernel,
        grid=(m // tm,),
        in_specs=[_row_spec(tm, d), _row_spec(tm, d), _const_spec((1, d)), _const_spec((1, d)),
                  _mod_spec(sh, tm), _mod_spec(sc, tm), _mod_spec(gate, tm), _const_spec((d, 2 * d))],
        out_specs=_row_spec(tm, d),
        out_shape=jax.ShapeDtypeStruct((m, d), F32),
        compiler_params=_cparams(("arbitrary",)),
        name="glu",
    )(x, y_ssm, g, d_skip, sh, sc, gate, w_glu)


def _glu_chunked_kernel(x_ref, y_ref, g_ref, d_ref, sh_ref, sc_ref, gate_ref, w_ref, o_ref, y_scr):
    seg, rps, d = x_ref.shape
    nb = d // V7X_LANES
    for k in range(nb):
        for j in range(S5_CHUNK):
            cols = slice(j * V7X_LANES, (j + 1) * V7X_LANES)
            for c in range(0, rps // S5_CHUNK, 2):
                pair = y_ref[k, c * seg:(c + 2) * seg, cols].astype(F32)
                y_scr[k, _chunk_rows(j, c, rps), :] = pair[:seg]
                y_scr[k, _chunk_rows(j, c + 1, rps), :] = pair[seg:]
    y_ssm = jnp.concatenate([y_scr[k] for k in range(nb)], axis=1)
    out = _glu_body(x_ref[...].reshape(seg * rps, d), y_ssm, g_ref, d_ref, sh_ref, sc_ref,
                    gate_ref, w_ref)
    o_ref[...] = out.reshape(seg, rps, d)


def _glu_chunked(x, y_chunked, g, d_skip, sh, sc, gate, w_glu):
    l, d = x.shape
    seg, rps, t = S5_SEGMENTS, S5_ROWS_PER_SEG, S5_CHUNK
    nb = d // V7X_LANES
    x_spec = pl.BlockSpec((seg, rps, d), lambda i: (0, i, 0))
    out = pl.pallas_call(
        _glu_chunked_kernel,
        grid=(l // (seg * rps),),
        in_specs=[x_spec, pl.BlockSpec((nb, seg * rps // t, t * V7X_LANES), lambda i: (0, i, 0)),
                  _const_spec((1, d)), _const_spec((1, d)), _const_spec((1, d)), _const_spec((1, d)),
                  _const_spec((1, d)), _const_spec((d, 2 * d))],
        out_specs=x_spec,
        out_shape=jax.ShapeDtypeStruct((seg, l // seg, d), F32),
        scratch_shapes=[pltpu.VMEM((nb, seg * rps, V7X_LANES), F32)],
        compiler_params=_cparams(("arbitrary",)),
        name="glu_chunked",
    )(x.reshape(seg, l // seg, d), y_chunked, g, d_skip, sh, sc, gate, w_glu)
    return out.reshape(l, d)


def _mlp_kernel(x_ref, g_ref, sh_ref, sc_ref, gate_ref, w1_ref, w2_ref, o_ref, *, ff_chunk):
    x = x_ref[...]
    xn = ((_rms_hat(x) * g_ref[...]) * (1.0 + sc_ref[...]) + sh_ref[...]).astype(BF16)
    acc = jnp.zeros(x.shape, F32)
    for c in range(w1_ref.shape[1] // ff_chunk):
        a = jnp.dot(xn, w1_ref[:, c * ff_chunk:(c + 1) * ff_chunk], preferred_element_type=F32)
        a = jnp.square(jnp.maximum(a, 0.0)).astype(BF16)
        acc = acc + jnp.dot(a, w2_ref[c * ff_chunk:(c + 1) * ff_chunk, :], preferred_element_type=F32)
    o_ref[...] = x + gate_ref[...] * acc


def _mlp(x, g, sh, sc, gate, w1, w2, tm, ff_chunk=1024):
    m, d = x.shape
    dff = w1.shape[1]
    return pl.pallas_call(
        functools.partial(_mlp_kernel, ff_chunk=ff_chunk),
        grid=(m // tm,),
        in_specs=[_row_spec(tm, d), _const_spec((1, d)), _mod_spec(sh, tm), _mod_spec(sc, tm),
                  _mod_spec(gate, tm), _const_spec((d, dff)), _const_spec((dff, d))],
        out_specs=_row_spec(tm, d),
        out_shape=jax.ShapeDtypeStruct((m, d), F32),
        compiler_params=_cparams(("arbitrary",)),
        name="mlp",
    )(x, g, sh, sc, gate, w1, w2)


def _head_norm(t, gain, e_ref, et_ref):
    msq = jnp.dot((t * t).astype(BF16), e_ref[...], preferred_element_type=F32)
    r = lax.rsqrt(msq + EPS)
    r_hi = r.astype(BF16)
    r_lo = (r - r_hi.astype(F32)).astype(BF16)
    r_full = (jnp.dot(r_hi, et_ref[...], preferred_element_type=F32)
              + jnp.dot(r_lo, et_ref[...], preferred_element_type=F32))
    return (t * r_full) * gain


def _qkv_kernel(h_ref, gkv_ref, shkv_ref, sckv_ref, gq_ref, shq_ref, scq_ref,
                wkvf_ref, bf_ref, wq_ref, kn_ref, qn_ref, e_ref, et_ref,
                k_ref, v_ref, lf_ref, q_ref, *, q_scale):
    x = h_ref[...]
    dm = x.shape[1]
    xh = _rms_hat(x)
    xkv = ((xh * gkv_ref[...]) * (1.0 + sckv_ref[...]) + shkv_ref[...]).astype(BF16)
    xq = ((xh * gq_ref[...]) * (1.0 + scq_ref[...]) + shq_ref[...]).astype(BF16)
    proj = jnp.dot(xkv, wkvf_ref[...], preferred_element_type=F32)
    k_ref[...] = _head_norm(proj[:, :dm], kn_ref[...], e_ref, et_ref)
    v_ref[...] = proj[:, dm:2 * dm]
    lf = jax.nn.log_sigmoid(proj[:, 2 * dm:] + bf_ref[...])
    lf_ref[...] = lf[:, :lf_ref.shape[1]]
    q = _head_norm(jnp.dot(xq, wq_ref[...], preferred_element_type=F32), qn_ref[...], e_ref, et_ref)
    q_ref[...] = (q * q_scale).astype(q_ref.dtype)


def _qkv(h, g_kv, sh_kv, sc_kv, g_q, sh_q, sc_q, w_kvf, b_f, w_q, k_norm, q_norm, e, et, tm,
         q_scale):
    m, d = h.shape
    nkvf = w_kvf.shape[1]
    return pl.pallas_call(
        functools.partial(_qkv_kernel, q_scale=q_scale),
        grid=(m // tm,),
        in_specs=[_row_spec(tm, d),
                  _const_spec((1, d)), _mod_spec(sh_kv, tm), _mod_spec(sc_kv, tm),
                  _const_spec((1, d)), _mod_spec(sh_q, tm), _mod_spec(sc_q, tm),
                  _const_spec((d, nkvf)), _const_spec((1, V7X_LANES)), _const_spec((d, d)),
                  _const_spec((1, d)), _const_spec((1, d)),
                  _const_spec((d, V7X_LANES)), _const_spec((V7X_LANES, d))],
        out_specs=[_row_spec(tm, d), _row_spec(tm, d), _row_spec(tm, N_HEADS), _row_spec(tm, d)],
        out_shape=[jax.ShapeDtypeStruct((m, d), F32), jax.ShapeDtypeStruct((m, d), F32),
                   jax.ShapeDtypeStruct((m, N_HEADS), F32), jax.ShapeDtypeStruct((m, d), BF16)],
        compiler_params=_cparams(("arbitrary",)),
        name="qkv",
    )(h, g_kv, sh_kv, sc_kv, g_q, sh_q, sc_q, w_kvf, b_f, w_q, k_norm, q_norm, e, et)


def _cumsum_kernel(x_ref, o_ref, carry_ref):
    @pl.when(pl.program_id(0) == 0)
    def _():
        carry_ref[...] = jnp.zeros_like(carry_ref)
    tm = x_ref.shape[0]
    tri = (lax.broadcasted_iota(jnp.int32, (tm, tm), 1)
           <= lax.broadcasted_iota(jnp.int32, (tm, tm), 0)).astype(F32)
    f = jnp.dot(tri, x_ref[...], precision=HIGHEST, preferred_element_type=F32) + carry_ref[...]
    o_ref[...] = f
    carry_ref[...] = f[tm - 1:tm, :]


def _cumsum_rows(x, tm):
    m, n = x.shape
    return pl.pallas_call(
        _cumsum_kernel,
        grid=(m // tm,),
        in_specs=[_row_spec(tm, n)],
        out_specs=_row_spec(tm, n),
        out_shape=jax.ShapeDtypeStruct((m, n), F32),
        scratch_shapes=[pltpu.VMEM((1, n), F32)],
        compiler_params=_cparams(("arbitrary",)),
        name="logf_cumsum",
    )(x)


LOG2E = 1.4426950408889634
ATTN_AUG = 3
ATTN_VROWS = 80


def _attn_kernel(q_ref, k_ref, v_ref, o_ref, sa_scr, sb_scr, m_scr, acc_scr, *, hd):
    i = pl.program_id(1)
    q_t = q_ref[...]
    m_scr[...] = jnp.full(m_scr.shape, -jnp.inf, F32)
    acc_scr[...] = jnp.zeros(acc_scr.shape, F32)

    def scores(j, dst):
        dst[...] = jnp.dot(k_ref[j], q_t, preferred_element_type=F32)

    def consume(j, src, masked):
        s_t = src[...]
        if masked:
            key = lax.broadcasted_iota(jnp.int32, s_t.shape, 0)
            qry = lax.broadcasted_iota(jnp.int32, s_t.shape, 1)
            s_t = jnp.where(key <= qry, s_t, -jnp.inf)
        m_old = m_scr[...]
        m_new = jnp.maximum(m_old, jnp.max(s_t, axis=0, keepdims=True))
        alpha = jnp.exp2(m_old - m_new)
        p_t = jnp.exp2(s_t - m_new).astype(BF16)
        acc_scr[...] = alpha * acc_scr[...] + jnp.dot(v_ref[j], p_t, preferred_element_type=F32)
        m_scr[...] = m_new

    scores(0, sa_scr)

    def pair(jj, c):
        t0 = 2 * jj
        scores(t0 + 1, sb_scr)
        consume(t0, sa_scr, False)
        scores(t0 + 2, sa_scr)
        consume(t0 + 1, sb_scr, False)
        return c

    lax.fori_loop(0, i // 2, pair, 0)

    @pl.when(i % 2 == 0)
    def _():
        consume(i, sa_scr, True)

    @pl.when(i % 2 == 1)
    def _():
        scores(i, sb_scr)
        consume(i - 1, sa_scr, False)
        consume(i, sb_scr, True)

    acc = acc_scr[...]
    o_ref[...] = (acc[:hd] / acc[hd:hd + 1]).astype(o_ref.dtype)


def _trunc_bf16(x):
    bits = lax.bitcast_convert_type(x, jnp.uint32) & jnp.uint32(0xFFFF0000)
    return lax.bitcast_convert_type(bits, F32)


def _split3(x):
    hi = _trunc_bf16(x)
    r = x - hi
    mid = _trunc_bf16(r)
    lo = _trunc_bf16(r - mid)
    return hi.astype(BF16), mid.astype(BF16), lo.astype(BF16)


def _attn_prompt(q_bf16, k, v, fcum, tq):
    l, d = k.shape
    nh = N_HEADS
    hd = d // nh
    nt = l // tq
    q_t = q_bf16.reshape(l, nh, hd).transpose(1, 2, 0)
    q_aug = jnp.concatenate([q_t, jnp.ones((nh, ATTN_AUG, l), BF16),
                             jnp.zeros((nh, V7X_LANES - hd - ATTN_AUG, l), BF16)], axis=1)
    kh = k.astype(BF16).reshape(l, nh, hd).transpose(1, 0, 2)
    f3 = jnp.stack(_split3(-fcum.T * LOG2E), axis=-1)
    k_aug = jnp.concatenate([kh, f3, jnp.zeros((nh, l, V7X_LANES - hd - ATTN_AUG), BF16)], axis=-1)
    k_aug = k_aug.reshape(nh, nt, tq, V7X_LANES)
    v_t = v.astype(BF16).reshape(l, nh, hd).transpose(1, 2, 0)
    v_aug = jnp.concatenate([v_t, jnp.ones((nh, 1, l), BF16),
                             jnp.zeros((nh, ATTN_VROWS - hd - 1, l), BF16)], axis=1)
    v_aug = v_aug.reshape(nh, ATTN_VROWS, nt, tq).transpose(0, 2, 1, 3)
    o_t = pl.pallas_call(
        functools.partial(_attn_kernel, hd=hd),
        grid=(nh, nt),
        in_specs=[pl.BlockSpec((None, V7X_LANES, tq), lambda h, i: (h, 0, i)),
                  pl.BlockSpec((None, nt, tq, V7X_LANES), lambda h, i: (h, 0, 0, 0)),
                  pl.BlockSpec((None, nt, ATTN_VROWS, tq), lambda h, i: (h, 0, 0, 0))],
        out_specs=pl.BlockSpec((None, hd, tq), lambda h, i: (h, 0, i)),
        out_shape=jax.ShapeDtypeStruct((nh, hd, l), BF16),
        scratch_shapes=[pltpu.VMEM((tq, tq), F32), pltpu.VMEM((tq, tq), F32),
                        pltpu.VMEM((1, tq), F32), pltpu.VMEM((ATTN_VROWS, tq), F32)],
        compiler_params=_cparams(("arbitrary", "arbitrary")),
        name="attn_prompt",
    )(q_aug, k_aug, v_aug)
    return o_t.transpose(2, 0, 1).reshape(l, d)


def _attn_paged_kernel(pt_ref, qb_ref, knew_ref, vnew_ref, lfnew_ref, *refs, pages_per_step, hd):
    del pt_ref
    pps = pages_per_step
    k_refs = refs[:pps]
    v_refs = refs[pps:2 * pps]
    lf_refs = refs[2 * pps:3 * pps]
    o_ref, m_scr, l_scr, f_scr, s_scr, p_scr, a_scr, acc_scr = refs[3 * pps:]
    nh = N_HEADS
    d, ps = acc_scr.shape
    step = pl.program_id(1)

    @pl.when(step == 0)
    def _():
        m_scr[...] = jnp.full(m_scr.shape, -jnp.inf, F32)
        l_scr[...] = jnp.zeros(l_scr.shape, F32)
        f_scr[...] = jnp.zeros(f_scr.shape, F32)
        acc_scr[...] = jnp.zeros(acc_scr.shape, F32)

    def page(k_rows, v_rows, f, n_valid):
        for h in range(nh):
            rows = slice(h * hd, (h + 1) * hd)
            s_scr[h:h + 1, :] = jnp.sum(k_rows(rows) * qb_ref[rows, :], axis=0, keepdims=True)
        z = s_scr[...] - f
        if n_valid is not None:
            z = jnp.where(lax.broadcasted_iota(jnp.int32, z.shape, 1) < n_valid, z, -jnp.inf)
        m_old = m_scr[...]
        m_new = jnp.maximum(m_old, jnp.max(z, axis=1, keepdims=True))
        alpha = jnp.exp(m_old - m_new)
        p = jnp.exp(z - m_new)
        l_scr[...] = alpha * l_scr[...] + jnp.sum(p, axis=1, keepdims=True)
        m_scr[...] = m_new
        p_scr[...] = p
        a_scr[...] = jnp.broadcast_to(alpha, (nh, ps))
        for h in range(nh):
            rows = slice(h * hd, (h + 1) * hd)
            p_h = jnp.broadcast_to(p_scr[h:h + 1, :], (hd, ps))
            a_h = jnp.broadcast_to(a_scr[h:h + 1, :], (hd, ps))
            acc_scr[rows, :] = a_h * acc_scr[rows, :] + p_h * v_rows(rows)

    tri = (lax.broadcasted_iota(jnp.int32, (ps, ps), 0)
           <= lax.broadcasted_iota(jnp.int32, (ps, ps), 1)).astype(F32)
    for r in range(pps):
        f = jnp.dot(lf_refs[r][...], tri, precision=HIGHEST, preferred_element_type=F32) + f_scr[...]
        page(lambda rows, r=r: k_refs[r][rows, :], lambda rows, r=r: v_refs[r][rows, :], f, None)
        f_scr[...] = f[:, ps - 1:ps]

    @pl.when(step == pl.num_programs(1) - 1)
    def _():
        f_new = jnp.broadcast_to(f_scr[...] + lfnew_ref[...], (nh, ps))
        page(lambda rows: jnp.broadcast_to(knew_ref[rows, :], (hd, ps)),
             lambda rows: jnp.broadcast_to(vnew_ref[rows, :], (hd, ps)), f_new, 1)
        for h in range(nh):
            rows = slice(h * hd, (h + 1) * hd)
            o_h = jnp.sum(acc_scr[rows, :], axis=1, keepdims=True)
            o_ref[rows, :] = o_h / l_scr[h:h + 1, :]


def _attn_paged(q_bf16, k_new, v_new, lf_new, cache_k, cache_v, cache_logf, page_table,
                pages_per_step=8):
    b, d = k_new.shape
    nh = N_HEADS
    hd = d // nh
    n_pages = page_table.shape[1]
    n_phys, ps = cache_k.shape[0], cache_k.shape[1]
    pps = math.gcd(pages_per_step, n_pages)
    ck = cache_k.transpose(0, 2, 3, 1).reshape(n_phys, d, ps)
    cv = cache_v.transpose(0, 2, 3, 1).reshape(n_phys, d, ps)
    clf = cache_logf.transpose(0, 2, 1).astype(F32)
    qb = jnp.broadcast_to(q_bf16.astype(F32)[:, :, None], (b, d, ps))
    pt = page_table.reshape(-1).astype(jnp.int32)

    def page_map(r):
        return lambda s, g, pt_ref: (pt_ref[s * n_pages + g * pps + r], 0, 0)

    seq = lambda rows, width: pl.BlockSpec((None, rows, width), lambda s, g, pt_ref: (s, 0, 0))
    in_specs = ([seq(d, ps), seq(d, 1), seq(d, 1), seq(nh, 1)]
                + [pl.BlockSpec((None, d, ps), page_map(r)) for r in range(pps)]
                + [pl.BlockSpec((None, d, ps), page_map(r)) for r in range(pps)]
                + [pl.BlockSpec((None, nh, ps), page_map(r)) for r in range(pps)])
    out = pl.pallas_call(
        functools.partial(_attn_paged_kernel, pages_per_step=pps, hd=hd),
        grid_spec=pltpu.PrefetchScalarGridSpec(
            num_scalar_prefetch=1,
            grid=(b, n_pages // pps),
            in_specs=in_specs,
            out_specs=seq(d, 1),
            scratch_shapes=[pltpu.VMEM((nh, 1), F32), pltpu.VMEM((nh, 1), F32), pltpu.VMEM((nh, 1), F32),
                            pltpu.VMEM((nh, ps), F32), pltpu.VMEM((nh, ps), F32),
                            pltpu.VMEM((nh, ps), F32), pltpu.VMEM((d, ps), F32)]),
        out_shape=jax.ShapeDtypeStruct((b, d, 1), F32),
        compiler_params=_cparams(("arbitrary", "arbitrary")),
        name="attn_paged",
    )(pt, qb, k_new.reshape(b, d, 1), v_new.reshape(b, d, 1), lf_new.reshape(b, nh, 1),
      *([ck] * pps), *([cv] * pps), *([clf] * pps))
    return out.reshape(b, d).astype(BF16)


def _oproj_kernel(o_ref, h_ref, gate_ref, w_ref, out_ref):
    out_ref[...] = h_ref[...] + gate_ref[...] * jnp.dot(o_ref[...], w_ref[...],
                                                        preferred_element_type=F32)


def _oproj(o, h, gate, w_o, tm):
    m, d = h.shape
    return pl.pallas_call(
        _oproj_kernel,
        grid=(m // tm,),
        in_specs=[_row_spec(tm, d), _row_spec(tm, d), _mod_spec(gate, tm), _const_spec((d, d))],
        out_specs=_row_spec(tm, d),
        out_shape=jax.ShapeDtypeStruct((m, d), F32),
        compiler_params=_cparams(("arbitrary",)),
        name="oproj",
    )(o, h, gate, w_o)


def _row_tile(m):
    return min(m, 512)


def _layer_mods(mod_all, kv_all, rows, d):
    per_layer = [[mod_all[layer, rows, i * d:(i + 1) * d] for i in range(6)]
                 for layer in range(mod_all.shape[0])]
    kv = [kv_all[0, rows, i * d:(i + 1) * d] for i in range(2)]
    return per_layer, kv


def kernel(x_prompt, x_sample, c_prompt, c_sample, state_ssm_re, state_ssm_im, cache_k, cache_v, cache_logf, page_table, ada_w, ada_b, g_mix, g_mlp, mlp_w1, mlp_w2, ssm_lam_re, ssm_lam_im, ssm_log_dt, ssm_b_re, ssm_b_im, ssm_c_re, ssm_c_im, ssm_d, w_glu, kv_ada_w, kv_ada_b, g_kv, w_kvf, b_f, k_norm, w_q, q_norm, w_o):
    bp, l, d = x_prompt.shape
    bs = x_sample.shape[0]
    nh = N_HEADS
    hd = d // nh
    assert bp == 1 and x_sample.shape[1] == 1 and ada_w.shape[0] == 2
    assert l % (S5_SEGMENTS * S5_ROWS_PER_SEG) == 0 and S5_ROWS_PER_SEG % (2 * S5_CHUNK) == 0

    row = lambda a: a.reshape(1, -1).astype(F32)
    w_glu_b = w_glu[0].astype(BF16)
    w1_b = mlp_w1.astype(BF16)
    w2_b = mlp_w2.astype(BF16)
    w_q_b = w_q[0].astype(BF16)
    w_o_b = w_o[0].astype(BF16)
    w_kvf_b = jnp.concatenate(
        [w_kvf, jnp.zeros((d, V7X_LANES - nh), w_kvf.dtype)], axis=1).astype(BF16)
    b_f_p = jnp.concatenate([b_f.astype(F32), jnp.zeros((V7X_LANES - nh,), F32)]).reshape(1, -1)
    head_of = jnp.arange(d) // hd
    e = (head_of[:, None] == jnp.arange(V7X_LANES)[None, :]).astype(F32)
    e_mean = (e / hd).astype(BF16)
    e_t = e.T.astype(BF16)
    k_norm_full = jnp.tile(k_norm.astype(F32), nh).reshape(1, d)
    q_norm_full = jnp.tile(q_norm[0].astype(F32), nh).reshape(1, d)

    ab_re, ab_im, bb_re, bb_im = _s5_discretise(ssm_lam_re[0], ssm_lam_im[0], ssm_log_dt[0],
                                                ssm_b_re[0], ssm_b_im[0])
    n_local = l // S5_CHUNK // S5_SEGMENTS
    s5_w = _s5_chunk_weights(ab_re, ab_im, bb_re, bb_im, ssm_c_re[0], ssm_c_im[0], n_local)

    n_c = bp + bs
    c_rows = -(-n_c // 8) * 8
    c_all = jnp.concatenate([c_prompt, c_sample, jnp.zeros((c_rows - n_c, d), F32)], axis=0)
    mod_all = _mods(c_all, ada_w, ada_b.reshape(ada_b.shape[0], 1, -1))
    kv_all = _mods(c_all, kv_ada_w[None], kv_ada_b.reshape(1, 1, -1))

    def trunk(x, rows, s5_fn, attend, q_scale):
        m = x.shape[0]
        tm = _row_tile(m)
        (l0, l1), (sh_kv, sc_kv) = _layer_mods(mod_all, kv_all, rows, d)
        sh_m, sc_m, g_m, sh_f, sc_f, g_f = l0
        h, st_re, st_im = s5_fn(x, row(g_mix[0]), sh_m, sc_m, g_m, tm)
        h = _mlp(h, row(g_mlp[0]), sh_f, sc_f, g_f, w1_b[0], w2_b[0], tm)
        sh_m, sc_m, g_m, sh_f, sc_f, g_f = l1
        k, v, lf, q = _qkv(h, row(g_kv), sh_kv, sc_kv, row(g_mix[1]), sh_m, sc_m,
                           w_kvf_b, b_f_p, w_q_b, k_norm_full, q_norm_full, e_mean, e_t, tm, q_scale)
        o = attend(q, k, v, lf)
        h = _oproj(o, h, g_m, w_o_b, tm)
        h = _mlp(h, row(g_mlp[1]), sh_f, sc_f, g_f, w1_b[1], w2_b[1], tm)
        return h, st_re, st_im, k, v, lf

    def s5_prompt(x, g, sh, sc, gate, tm):
        y, st_re, st_im = _s5_prompt(_ada_rms_chunked(x, g, sh, sc), s5_w)
        return _glu_chunked(x, y, g, row(ssm_d[0]), sh, sc, gate, w_glu_b), st_re, st_im

    def attend_prompt(q, k, v, lf):
        fcum = _cumsum_rows(lf, _row_tile(l))
        return _attn_prompt(q, k, v, fcum, min(l, 512))

    y_p, sre_p, sim_p, k_p, v_p, f_p = trunk(x_prompt[0], slice(0, 1), s5_prompt, attend_prompt,
                                             LOG2E / math.sqrt(hd))

    def s5_sample(x, g, sh, sc, gate, tm):
        u = _ada_rms(x, g, sh, sc, tm, F32)
        y, st_re, st_im = _s5_step(u, state_ssm_re[0], state_ssm_im[0], ab_re, ab_im, bb_re, bb_im,
                                   ssm_c_re[0], ssm_c_im[0])
        return _glu(x, y, g, row(ssm_d[0]), sh, sc, gate, w_glu_b, tm), st_re, st_im

    def attend_sample(q, k, v, lf):
        return _attn_paged(q, k, v, lf, cache_k, cache_v, cache_logf, page_table)

    y_s, sre_s, sim_s, k_s, v_s, f_s = trunk(x_sample[:, 0], slice(bp, bp + bs), s5_sample,
                                             attend_sample, 1.0 / math.sqrt(hd))

    sdt = state_ssm_re.dtype
    return (y_p[None], y_s[:, None],
            sre_p[None, None].astype(sdt), sim_p[None, None].astype(sdt),
            k_p.reshape(1, l, nh, hd), v_p.reshape(1, l, nh, hd), f_p[None],
            sre_s[None].astype(sdt), sim_s[None].astype(sdt),
            k_s.reshape(bs, 1, nh, hd), v_s.reshape(bs, 1, nh, hd), f_s[:, None])
```

```python
import functools
import math

import jax
import jax.numpy as jnp
from jax import lax
from jax.experimental import pallas as pl
from jax.experimental.pallas import tpu as pltpu

F32 = jnp.float32
BF16 = jnp.bfloat16
HIGHEST = lax.Precision.HIGHEST

EPS = 1e-6
GROUP_SIZE = 16
STATE_DIM = 64
N_HEADS = 16
PAGE_SIZE = 128
S5_CHUNK = 8
S5_SEGMENTS = 8
S5_ROWS_PER_SEG = 64
V7X_LANES = 128
V7X_VMEM_LIMIT = 56 * 1024 * 1024


def _cparams(sem):
    return pltpu.CompilerParams(dimension_semantics=sem, vmem_limit_bytes=V7X_VMEM_LIMIT)


def _const_spec(shape):
    nd = len(shape)
    return pl.BlockSpec(shape, lambda *_: (0,) * nd, pipeline_mode=pl.Buffered(1))


def _row_spec(tm, width):
    return pl.BlockSpec((tm, width), lambda i: (i, 0))


def _mod_spec(arr, tm):
    if arr.shape[0] == 1:
        return pl.BlockSpec((1, arr.shape[1]), lambda i: (0, 0))
    return pl.BlockSpec((tm, arr.shape[1]), lambda i: (i, 0))


def _rms_hat(x):
    return x * lax.rsqrt(jnp.mean(x * x, axis=-1, keepdims=True) + EPS)


def _mods_kernel(c_ref, w_ref, b_ref, o_ref):
    o_ref[...] = jnp.dot(c_ref[...], w_ref[...], precision=HIGHEST,
                         preferred_element_type=F32) + b_ref[...]


def _mods(c, w3, b3, tn=2048):
    nl, d, n = w3.shape
    mp = c.shape[0]
    return pl.pallas_call(
        _mods_kernel,
        grid=(nl, n // tn),
        in_specs=[pl.BlockSpec((mp, d), lambda l, j: (0, 0)),
                  pl.BlockSpec((None, d, tn), lambda l, j: (l, 0, j)),
                  pl.BlockSpec((None, 1, tn), lambda l, j: (l, 0, j))],
        out_specs=pl.BlockSpec((None, mp, tn), lambda l, j: (l, 0, j)),
        out_shape=jax.ShapeDtypeStruct((nl, mp, n), F32),
        compiler_params=_cparams(("arbitrary", "arbitrary")),
        name="mods",
    )(c, w3, b3)


def _ada_rms_kernel(x_ref, g_ref, sh_ref, sc_ref, o_ref):
    u = (_rms_hat(x_ref[...]) * g_ref[...]) * (1.0 + sc_ref[...]) + sh_ref[...]
    o_ref[...] = u.astype(o_ref.dtype)


def _ada_rms(x, g, sh, sc, tm, out_dtype):
    m, d = x.shape
    return pl.pallas_call(
        _ada_rms_kernel,
        grid=(m // tm,),
        in_specs=[_row_spec(tm, d), _const_spec((1, d)), _mod_spec(sh, tm), _mod_spec(sc, tm)],
        out_specs=_row_spec(tm, d),
        out_shape=jax.ShapeDtypeStruct((m, d), out_dtype),
        compiler_params=_cparams(("arbitrary",)),
        name="ada_rms",
    )(x, g, sh, sc)


def _chunk_rows(j, c, rows_per_seg):
    return pl.ds(c * S5_CHUNK + j, S5_SEGMENTS, stride=rows_per_seg)


def _ada_rms_chunked_kernel(x_ref, g_ref, sh_ref, sc_ref, o_ref, u_scr):
    seg, rps, d = x_ref.shape
    x = x_ref[...].reshape(seg * rps, d)
    u = (_rms_hat(x) * g_ref[...]) * (1.0 + sc_ref[...]) + sh_ref[...]
    for k in range(d // V7X_LANES):
        u_scr[k] = u[:, k * V7X_LANES:(k + 1) * V7X_LANES]
        for j in range(S5_CHUNK):
            cols = slice(j * V7X_LANES, (j + 1) * V7X_LANES)
            for c in range(0, rps // S5_CHUNK, 2):
                pair = jnp.concatenate([u_scr[k, _chunk_rows(j, c, rps), :],
                                        u_scr[k, _chunk_rows(j, c + 1, rps), :]], axis=0)
                o_ref[k, c * seg:(c + 2) * seg, cols] = pair.astype(o_ref.dtype)


def _ada_rms_chunked(x, g, sh, sc):
    l, d = x.shape
    seg, rps, t = S5_SEGMENTS, S5_ROWS_PER_SEG, S5_CHUNK
    nb = d // V7X_LANES
    return pl.pallas_call(
        _ada_rms_chunked_kernel,
        grid=(l // (seg * rps),),
        in_specs=[pl.BlockSpec((seg, rps, d), lambda i: (0, i, 0)), _const_spec((1, d)),
                  _const_spec((1, d)), _const_spec((1, d))],
        out_specs=pl.BlockSpec((nb, seg * rps // t, t * V7X_LANES), lambda i: (0, i, 0)),
        out_shape=jax.ShapeDtypeStruct((nb, l // t, t * V7X_LANES), BF16),
        scratch_shapes=[pltpu.VMEM((nb, seg * rps, V7X_LANES), F32)],
        compiler_params=_cparams(("arbitrary",)),
        name="ada_rms_chunked",
    )(x.reshape(seg, l // seg, d), g, sh, sc)


def _s5_discretise(lam_re, lam_im, log_dt, b_re, b_im):
    dt = jnp.exp(log_dt.astype(F32))[:, None]
    lr = lam_re.astype(F32)
    li = lam_im.astype(F32)
    mag = jnp.exp(lr * dt)
    ang = li * dt
    ab_re = mag * jnp.cos(ang)
    ab_im = mag * jnp.sin(ang)
    den = lr * lr + li * li
    zr = ((ab_re - 1) * lr + ab_im * li) / den
    zi = (ab_im * lr - (ab_re - 1) * li) / den
    br = b_re.astype(F32)
    bi = b_im.astype(F32)
    bb_re = zr[..., None] * br - zi[..., None] * bi
    bb_im = zr[..., None] * bi + zi[..., None] * br
    return ab_re, ab_im, bb_re, bb_im


def _cpow_table(ar, ai, n):
    tr = jnp.ones((1,) + ar.shape, F32)
    ti = jnp.zeros((1,) + ar.shape, F32)
    br, bi = ar, ai
    while tr.shape[0] < n:
        tr, ti = (jnp.concatenate([tr, tr * br - ti * bi], axis=0),
                  jnp.concatenate([ti, tr * bi + ti * br], axis=0))
        br, bi = br * br - bi * bi, 2.0 * br * bi
    return tr[:n], ti[:n]


def _block_lanes(x_re, x_im, nb):
    def blk(x):
        x = x.reshape((nb, x.shape[0] // nb) + x.shape[1:])
        x = jnp.moveaxis(x, 1, -2)
        return x.reshape(x.shape[:-2] + (x.shape[-2] * x.shape[-1],))
    return jnp.concatenate([blk(x_re), blk(x_im)], axis=-1)


_ein = functools.partial(jnp.einsum, precision=HIGHEST)


def _expand_groups(compact, row_w, col_w, gb):
    nb, r, c = compact.shape
    src = jnp.arange(c)
    dst = jnp.arange(c * gb)
    dst_src = (dst // (col_w * gb)) * col_w + dst % col_w
    rep = (src[:, None] == dst_src[None, :]).astype(BF16)
    wide = jnp.einsum('krc,cd->krd', compact.astype(BF16), rep, preferred_element_type=BF16)
    row_group = (jnp.arange(r) // row_w) % gb
    col_group = (dst // col_w) % gb
    return jnp.where(row_group[:, None] == col_group[None, :], wide, jnp.zeros((), BF16))


def _s5_chunk_weights(ab_re, ab_im, bb_re, bb_im, c_re, c_im, n_local):
    t = S5_CHUNK
    g, p = ab_re.shape
    h = GROUP_SIZE
    gb = V7X_LANES // h
    nb = g // gb
    cr = c_re.astype(F32)
    ci = c_im.astype(F32)
    pw_re, pw_im = _cpow_table(ab_re, ab_im, t + 1)
    cb_rr = _ein('gop,gph->gpoh', cr, bb_re) - _ein('gop,gph->gpoh', ci, bb_im)
    cb_ii = _ein('gop,gph->gpoh', cr, bb_im) + _ein('gop,gph->gpoh', ci, bb_re)
    kern = _ein('tgp,gpoh->gtoh', pw_re[:t], cb_rr) - _ein('tgp,gpoh->gtoh', pw_im[:t], cb_ii)
    tau = jnp.arange(t)[None, :] - jnp.arange(t)[:, None]
    m = jnp.where((tau >= 0)[None, :, :, None, None], kern[:, jnp.clip(tau, 0, t - 1)], 0.0)
    m = m.reshape(nb, gb, t, t, h, h)
    m_all = _expand_groups(m.transpose(0, 2, 1, 5, 3, 4).reshape(nb, t * gb * h, t * h), h, h, gb)
    rev = t - 1 - jnp.arange(t)
    p_re = _ein('jgp,gph->gjhp', pw_re[rev], bb_re) - _ein('jgp,gph->gjhp', pw_im[rev], bb_im)
    p_im = _ein('jgp,gph->gjhp', pw_re[rev], bb_im) + _ein('jgp,gph->gjhp', pw_im[rev], bb_re)
    p_ri = jnp.stack([p_re, p_im], axis=1).reshape(nb, gb, 2, t, h, p)
    p_all = _expand_groups(p_ri.transpose(0, 3, 1, 4, 2, 5).reshape(nb, t * gb * h, 2 * p), h, p, gb)
    w_re = _ein('gop,tgp->gpto', cr, pw_re[1:]) - _ein('gop,tgp->gpto', ci, pw_im[1:])
    w_im = _ein('gop,tgp->gpto', cr, pw_im[1:]) + _ein('gop,tgp->gpto', ci, pw_re[1:])
    w_ri = jnp.stack([w_re, -w_im], axis=1).reshape(nb, gb, 2, p, t, h)
    q_all = _expand_groups(w_ri.transpose(0, 2, 1, 3, 4, 5).reshape(nb, 2 * gb * p, t * h), p, h, gb)
    at_re, at_im = pw_re[t], pw_im[t]
    lp_re, lp_im = _cpow_table(at_re, at_im, n_local + 1)
    apow = _block_lanes(jnp.moveaxis(lp_re[:n_local], 0, 1), jnp.moveaxis(lp_im[:n_local], 0, 1), nb)
    a_chunk = _block_lanes(at_re[:, None], at_im[:, None], nb)
    a_seg = _block_lanes(lp_re[n_local][:, None], lp_im[n_local][:, None], nb)
    return m_all.astype(BF16), p_all.astype(BF16), q_all.astype(BF16), a_chunk, apow, a_seg


def _s5_prompt_kernel(u_ref, p_ref, m_ref, q_ref, ac_ref, apow_ref, aseg_ref,
                      y_ref, fin_ref, sp_scr, *, n_local):
    seg = S5_SEGMENTS
    half = sp_scr.shape[1] // 2
    sp_scr[...] = jnp.dot(u_ref[...], p_ref[...], preferred_element_type=F32)
    ar = jnp.broadcast_to(ac_ref[:, :half], (seg, half))
    ai = jnp.broadcast_to(ac_ref[:, half:], (seg, half))

    def scan_step(k, carry):
        sr, si = carry
        rows = pl.ds(pl.multiple_of(k * seg, seg), seg)
        gr = sp_scr[rows, :half]
        gi = sp_scr[rows, half:]
        sp_scr[rows, :half] = sr
        sp_scr[rows, half:] = si
        return (ar * sr - ai * si + gr, ar * si + ai * sr + gi)

    zero = jnp.zeros((seg, half), F32)
    er, ei = lax.fori_loop(0, n_local, scan_step, (zero, zero))

    sgr = aseg_ref[:, :half]
    sgi = aseg_ref[:, half:]
    cr = jnp.zeros((1, half), F32)
    ci = jnp.zeros((1, half), F32)
    rows_r, rows_i = [], []
    for s in range(seg):
        rows_r.append(cr)
        rows_i.append(ci)
        cr, ci = (sgr * cr - sgi * ci + er[s:s + 1], sgr * ci + sgi * cr + ei[s:s + 1])
    fin_ref[...] = jnp.concatenate([cr, ci], axis=-1)
    sin_r = jnp.concatenate(rows_r, axis=0)
    sin_i = jnp.concatenate(rows_i, axis=0)

    def fix_step(k, c):
        rows = pl.ds(pl.multiple_of(k * seg, seg), seg)
        apk = apow_ref[pl.ds(k, 1), :]
        pr = jnp.broadcast_to(apk[:, :half], (seg, half))
        pi = jnp.broadcast_to(apk[:, half:], (seg, half))
        sp_scr[rows, :half] += pr * sin_r - pi * sin_i
        sp_scr[rows, half:] += pr * sin_i + pi * sin_r
        return c

    lax.fori_loop(0, n_local, fix_step, 0)
    y = (jnp.dot(sp_scr[...].astype(BF16), q_ref[...], preferred_element_type=F32)
         + jnp.dot(u_ref[...], m_ref[...], preferred_element_type=F32))
    y_ref[...] = y.astype(y_ref.dtype)


def _s5_prompt(u_chunked, weights):
    m_all, p_all, q_all, a_chunk, apow, a_seg = weights
    nb, nc, kw = u_chunked.shape
    n_local = nc // S5_SEGMENTS
    sw = p_all.shape[2]
    blk = lambda rows, cols: pl.BlockSpec((None, rows, cols), lambda k: (k, 0, 0))
    yt, fin = pl.pallas_call(
        functools.partial(_s5_prompt_kernel, n_local=n_local),
        grid=(nb,),
        in_specs=[blk(nc, kw), blk(kw, sw), blk(kw, kw), blk(sw, kw), blk(1, sw), blk(n_local, sw),
                  blk(1, sw)],
        out_specs=[blk(nc, kw), blk(1, sw)],
        out_shape=[jax.ShapeDtypeStruct((nb, nc, kw), BF16), jax.ShapeDtypeStruct((nb, 1, sw), F32)],
        scratch_shapes=[pltpu.VMEM((nc, sw), F32)],
        compiler_params=_cparams(("arbitrary",)),
        name="s5_prompt",
    )(u_chunked, p_all, m_all, q_all, a_chunk, apow, a_seg)
    fin = fin.reshape(nb, 2, -1, STATE_DIM)
    return yt, fin[:, 0].reshape(-1, STATE_DIM), fin[:, 1].reshape(-1, STATE_DIM)


def _s5_step_kernel(u_ref, h_ref, hs_ref, a1_ref, a2_ref, bbt_ref, cq_ref, y_ref, hn_ref):
    bu = jnp.einsum('gbh,ghp->gbp', u_ref[...], bbt_ref[...], precision=HIGHEST,
                    preferred_element_type=F32)
    hn = a1_ref[...] * h_ref[...] + a2_ref[...] * hs_ref[...] + bu
    hn_ref[...] = hn
    y_ref[...] = jnp.einsum('gbp,gph->gbh', hn, cq_ref[...], precision=HIGHEST,
                            preferred_element_type=F32)


def _s5_step(u, h0_re, h0_im, ab_re, ab_im, bb_re, bb_im, c_re, c_im):
    b, d = u.shape
    g, p = ab_re.shape
    h = GROUP_SIZE
    u3 = u.reshape(b, g, h).transpose(1, 0, 2)
    hc = jnp.concatenate([h0_re, h0_im], axis=-1).transpose(1, 0, 2).astype(F32)
    hs = jnp.concatenate([h0_im, h0_re], axis=-1).transpose(1, 0, 2).astype(F32)
    a1 = jnp.concatenate([ab_re, ab_re], axis=-1)[:, None, :]
    a2 = jnp.concatenate([-ab_im, ab_im], axis=-1)[:, None, :]
    bbt = jnp.concatenate([bb_re, bb_im], axis=1).transpose(0, 2, 1)
    cq = jnp.concatenate([c_re.astype(F32), -c_im.astype(F32)], axis=-1).transpose(0, 2, 1)
    y3, hn = pl.pallas_call(
        _s5_step_kernel,
        out_shape=[jax.ShapeDtypeStruct((g, b, h), F32), jax.ShapeDtypeStruct((g, b, 2 * p), F32)],
        compiler_params=pltpu.CompilerParams(vmem_limit_bytes=V7X_VMEM_LIMIT),
        name="s5_step",
    )(u3, hc, hs, a1, a2, bbt, cq)
    y = y3.transpose(1, 0, 2).reshape(b, d)
    hn = hn.transpose(1, 0, 2)
    return y, hn[..., :p], hn[..., p:]


def _glu_body(x, y_ssm, g_ref, d_ref, sh_ref, sc_ref, gate_ref, w_ref):
    dm = x.shape[1]
    u = (_rms_hat(x) * g_ref[...]) * (1.0 + sc_ref[...]) + sh_ref[...]
    y = y_ssm + d_ref[...] * u
    z = jnp.dot(jax.nn.gelu(y).astype(BF16), w_ref[...], preferred_element_type=F32)
    out = z[:, :dm] * jax.nn.sigmoid(z[:, dm:])
    return x + gate_ref[...] * out


def _glu_kernel(x_ref, y_ref, g_ref, d_ref, sh_ref, sc_ref, gate_ref, w_ref, o_ref):
    o_ref[...] = _glu_body(x_ref[...], y_ref[...].astype(F32), g_ref, d_ref, sh_ref, sc_ref,
                           gate_ref, w_ref)


def _glu(x, y_ssm, g, d_skip, sh, sc, gate, w_glu, tm):
    m, d = x.shape
    return pl.pallas_call(
        _glu_kernel,
        grid=(m // tm,),
        in_specs=[_row_spec(tm, d), _row_spec(tm, d), _const_spec((1, d)), _const_spec((1, d)),
                  _mod_spec(sh, tm), _mod_spec(sc, tm), _mod_spec(gate, tm), _const_spec((d, 2 * d))],
        out_specs=_row_spec(tm, d),
        out_shape=jax.ShapeDtypeStruct((m, d), F32),
        compiler_params=_cparams(("arbitrary",)),
        name="glu",
    )(x, y_ssm, g, d_skip, sh, sc, gate, w_glu)


def _glu_chunked_kernel(x_ref, y_ref, g_ref, d_ref, sh_ref, sc_ref, gate_ref, w_ref, o_ref, y_scr):
    seg, rps, d = x_ref.shape
    nb = d // V7X_LANES
    for k in range(nb):
        for j in range(S5_CHUNK):
            cols = slice(j * V7X_LANES, (j + 1) * V7X_LANES)
            for c in range(0, rps // S5_CHUNK, 2):
                pair = y_ref[k, c * seg:(c + 2) * seg, cols].astype(F32)
                y_scr[k, _chunk_rows(j, c, rps), :] = pair[:seg]
                y_scr[k, _chunk_rows(j, c + 1, rps), :] = pair[seg:]
    y_ssm = jnp.concatenate([y_scr[k] for k in range(nb)], axis=1)
    out = _glu_body(x_ref[...].reshape(seg * rps, d), y_ssm, g_ref, d_ref, sh_ref, sc_ref,
                    gate_ref, w_ref)
    o_ref[...] = out.reshape(seg, rps, d)


def _glu_chunked(x, y_chunked, g, d_skip, sh, sc, gate, w_glu):
    l, d = x.shape
    seg, rps, t = S5_SEGMENTS, S5_ROWS_PER_SEG, S5_CHUNK
    nb = d // V7X_LANES
    x_spec = pl.BlockSpec((seg, rps, d), lambda i: (0, i, 0))
    out = pl.pallas_call(
        _glu_chunked_kernel,
        grid=(l // (seg * rps),),
        in_specs=[x_spec, pl.BlockSpec((nb, seg * rps // t, t * V7X_LANES), lambda i: (0, i, 0)),
                  _const_spec((1, d)), _const_spec((1, d)), _const_spec((1, d)), _const_spec((1, d)),
                  _const_spec((1, d)), _const_spec((d, 2 * d))],
        out_specs=x_spec,
        out_shape=jax.ShapeDtypeStruct((seg, l // seg, d), F32),
        scratch_shapes=[pltpu.VMEM((nb, seg * rps, V7X_LANES), F32)],
        compiler_params=_cparams(("arbitrary",)),
        name="glu_chunked",
    )(x.reshape(seg, l // seg, d), y_chunked, g, d_skip, sh, sc, gate, w_glu)
    return out.reshape(l, d)


def _mlp_kernel(x_ref, g_ref, sh_ref, sc_ref, gate_ref, w1_ref, w2_ref, o_ref, *, ff_chunk):
    x = x_ref[...]
    xn = ((_rms_hat(x) * g_ref[...]) * (1.0 + sc_ref[...]) + sh_ref[...]).astype(BF16)
    acc = jnp.zeros(x.shape, F32)
    for c in range(w1_ref.shape[1] // ff_chunk):
        a = jnp.dot(xn, w1_ref[:, c * ff_chunk:(c + 1) * ff_chunk], preferred_element_type=F32)
        a = jnp.square(jnp.maximum(a, 0.0)).astype(BF16)
        acc = acc + jnp.dot(a, w2_ref[c * ff_chunk:(c + 1) * ff_chunk, :], preferred_element_type=F32)
    o_ref[...] = x + gate_ref[...] * acc


def _mlp(x, g, sh, sc, gate, w1, w2, tm, ff_chunk=1024):
    m, d = x.shape
    dff = w1.shape[1]
    return pl.pallas_call(
        functools.partial(_mlp_kernel, ff_chunk=ff_chunk),
        grid=(m // tm,),
        in_specs=[_row_spec(tm, d), _const_spec((1, d)), _mod_spec(sh, tm), _mod_spec(sc, tm),
                  _mod_spec(gate, tm), _const_spec((d, dff)), _const_spec((dff, d))],
        out_specs=_row_spec(tm, d),
        out_shape=jax.ShapeDtypeStruct((m, d), F32),
        compiler_params=_cparams(("arbitrary",)),
        name="mlp",
    )(x, g, sh, sc, gate, w1, w2)


def _head_norm(t, gain, e_ref, et_ref):
    msq = jnp.dot((t * t).astype(BF16), e_ref[...], preferred_element_type=F32)
    r = lax.rsqrt(msq + EPS)
    r_hi = r.astype(BF16)
    r_lo = (r - r_hi.astype(F32)).astype(BF16)
    r_full = (jnp.dot(r_hi, et_ref[...], preferred_element_type=F32)
              + jnp.dot(r_lo, et_ref[...], preferred_element_type=F32))
    return (t * r_full) * gain


def _qkv_kernel(h_ref, gkv_ref, shkv_ref, sckv_ref, gq_ref, shq_ref, scq_ref,
                wkvf_ref, bf_ref, wq_ref, kn_ref, qn_ref, e_ref, et_ref,
                k_ref, v_ref, lf_ref, q_ref, *, q_scale):
    x = h_ref[...]
    dm = x.shape[1]
    xh = _rms_hat(x)
    xkv = ((xh * gkv_ref[...]) * (1.0 + sckv_ref[...]) + shkv_ref[...]).astype(BF16)
    xq = ((xh * gq_ref[...]) * (1.0 + scq_ref[...]) + shq_ref[...]).astype(BF16)
    proj = jnp.dot(xkv, wkvf_ref[...], preferred_element_type=F32)
    k_ref[...] = _head_norm(proj[:, :dm], kn_ref[...], e_ref, et_ref)
    v_ref[...] = proj[:, dm:2 * dm]
    lf = jax.nn.log_sigmoid(proj[:, 2 * dm:] + bf_ref[...])
    lf_ref[...] = lf[:, :lf_ref.shape[1]]
    q = _head_norm(jnp.dot(xq, wq_ref[...], preferred_element_type=F32), qn_ref[...], e_ref, et_ref)
    q_ref[...] = (q * q_scale).astype(q_ref.dtype)


def _trunc_bf16(x):
    bits = lax.bitcast_convert_type(x, jnp.uint32) & jnp.uint32(0xFFFF0000)
    return lax.bitcast_convert_type(bits, F32)


def _qkv_prompt_kernel(h_ref, gkv_ref, shkv_ref, sckv_ref, gq_ref, shq_ref, scq_ref,
                       wkvf_ref, bf_ref, wq_ref, kn_ref, qn_ref, e_ref, et_ref,
                       kt_ref, vt_ref, lft_ref, f_ref, qa_ref, ka_ref, va_ref, carry_scr, *, q_scale):
    @pl.when(pl.program_id(0) == 0)
    def _():
        carry_scr[...] = jnp.zeros_like(carry_scr)

    x = h_ref[...]
    tm, dm = x.shape
    nh = N_HEADS
    hd = dm // nh
    xh = _rms_hat(x)
    xkv = ((xh * gkv_ref[...]) * (1.0 + sckv_ref[...]) + shkv_ref[...]).astype(BF16)
    xq = ((xh * gq_ref[...]) * (1.0 + scq_ref[...]) + shq_ref[...]).astype(BF16)
    proj = jnp.dot(xkv, wkvf_ref[...], preferred_element_type=F32)
    k = _head_norm(proj[:, :dm], kn_ref[...], e_ref, et_ref)
    v_t = proj[:, dm:2 * dm].T
    lf = jax.nn.log_sigmoid(proj[:, 2 * dm:] + bf_ref[...])
    q = _head_norm(jnp.dot(xq, wq_ref[...], preferred_element_type=F32), qn_ref[...], e_ref, et_ref)
    q_t = (q * q_scale).T
    kt_ref[...] = k.T
    vt_ref[...] = v_t
    lft_ref[...] = lf.T[:nh]
    tri = (lax.broadcasted_iota(jnp.int32, (tm, tm), 1)
           <= lax.broadcasted_iota(jnp.int32, (tm, tm), 0)).astype(F32)
    f = jnp.dot(tri, lf, precision=HIGHEST, preferred_element_type=F32) + carry_scr[...]
    carry_scr[...] = f[tm - 1:tm, :]
    f_ref[...] = f[:, :nh]

    q_pad = (lax.broadcasted_iota(jnp.int32, (V7X_LANES - hd, tm), 0) < ATTN_AUG).astype(BF16)
    v_pad = (lax.broadcasted_iota(jnp.int32, (ATTN_VROWS - hd, tm), 0) < 1).astype(BF16)
    lane = lax.broadcasted_iota(jnp.int32, (tm, V7X_LANES), 1)
    fs = f * (-LOG2E)
    for h in range(nh):
        rows = slice(h * hd, (h + 1) * hd)
        qa_ref[h, :hd, :] = q_t[rows].astype(BF16)
        qa_ref[h, hd:, :] = q_pad
        va_ref[h, :hd, :] = v_t[rows].astype(BF16)
        va_ref[h, hd:, :] = v_pad
        two = (2 * hd) * (h // 2)
        base = k[:, two:two + 2 * hd]
        if h % 2:
            base = pltpu.roll(base, hd, axis=1)
        hi = _trunc_bf16(jnp.broadcast_to(fs[:, h:h + 1], (tm, V7X_LANES)))
        rest = jnp.broadcast_to(fs[:, h:h + 1], (tm, V7X_LANES)) - hi
        mid = _trunc_bf16(rest)
        lo = _trunc_bf16(rest - mid)
        aug = jnp.where(lane == hd, hi, jnp.where(lane == hd + 1, mid, jnp.where(lane == hd + 2, lo, 0.0)))
        ka_ref[h] = jnp.where(lane < hd, base, aug).astype(BF16)


def _qkv_prompt(h, g_kv, sh_kv, sc_kv, g_q, sh_q, sc_q, w_kvf, b_f, w_q, k_norm, q_norm, e, et, tm,
                q_scale):
    l, d = h.shape
    nh = N_HEADS
    nkvf = w_kvf.shape[1]
    nt = l // tm
    col = lambda rows: pl.BlockSpec((rows, tm), lambda i: (0, i))
    return pl.pallas_call(
        functools.partial(_qkv_prompt_kernel, q_scale=q_scale),
        grid=(nt,),
        in_specs=[_row_spec(tm, d),
                  _const_spec((1, d)), _const_spec((1, d)), _const_spec((1, d)),
                  _const_spec((1, d)), _const_spec((1, d)), _const_spec((1, d)),
                  _const_spec((d, nkvf)), _const_spec((1, V7X_LANES)), _const_spec((d, d)),
                  _const_spec((1, d)), _const_spec((1, d)),
                  _const_spec((d, V7X_LANES)), _const_spec((V7X_LANES, d))],
        out_specs=[col(d), col(d), col(nh), _row_spec(tm, nh),
                   pl.BlockSpec((nh, V7X_LANES, tm), lambda i: (0, 0, i)),
                   pl.BlockSpec((nh, tm, V7X_LANES), lambda i: (0, i, 0)),
                   pl.BlockSpec((nh, None, ATTN_VROWS, tm), lambda i: (0, i, 0, 0))],
        out_shape=[jax.ShapeDtypeStruct((d, l), F32), jax.ShapeDtypeStruct((d, l), F32),
                   jax.ShapeDtypeStruct((nh, l), F32), jax.ShapeDtypeStruct((l, nh), F32),
                   jax.ShapeDtypeStruct((nh, V7X_LANES, l), BF16),
                   jax.ShapeDtypeStruct((nh, l, V7X_LANES), BF16),
                   jax.ShapeDtypeStruct((nh, nt, ATTN_VROWS, tm), BF16)],
        scratch_shapes=[pltpu.VMEM((1, V7X_LANES), F32)],
        compiler_params=_cparams(("arbitrary",)),
        name="qkv_prompt",
    )(h, g_kv, sh_kv, sc_kv, g_q, sh_q, sc_q, w_kvf, b_f, w_q, k_norm, q_norm, e, et)


def _qkv(h, g_kv, sh_kv, sc_kv, g_q, sh_q, sc_q, w_kvf, b_f, w_q, k_norm, q_norm, e, et, tm,
         q_scale):
    m, d = h.shape
    nkvf = w_kvf.shape[1]
    return pl.pallas_call(
        functools.partial(_qkv_kernel, q_scale=q_scale),
        grid=(m // tm,),
        in_specs=[_row_spec(tm, d),
                  _const_spec((1, d)), _mod_spec(sh_kv, tm), _mod_spec(sc_kv, tm),
                  _const_spec((1, d)), _mod_spec(sh_q, tm), _mod_spec(sc_q, tm),
                  _const_spec((d, nkvf)), _const_spec((1, V7X_LANES)), _const_spec((d, d)),
                  _const_spec((1, d)), _const_spec((1, d)),
                  _const_spec((d, V7X_LANES)), _const_spec((V7X_LANES, d))],
        out_specs=[_row_spec(tm, d), _row_spec(tm, d), _row_spec(tm, N_HEADS), _row_spec(tm, d)],
        out_shape=[jax.ShapeDtypeStruct((m, d), F32), jax.ShapeDtypeStruct((m, d), F32),
                   jax.ShapeDtypeStruct((m, N_HEADS), F32), jax.ShapeDtypeStruct((m, d), BF16)],
        compiler_params=_cparams(("arbitrary",)),
        name="qkv",
    )(h, g_kv, sh_kv, sc_kv, g_q, sh_q, sc_q, w_kvf, b_f, w_q, k_norm, q_norm, e, et)


LOG2E = 1.4426950408889634
ATTN_AUG = 3
ATTN_VROWS = 80


def _attn_kernel(js_ref, q_ref, k_ref, v_ref, o_ref, sa_scr, sb_scr, m_scr, acc_scr, *, hd):
    i = pl.program_id(1)
    js = js_ref[pl.program_id(0) * pl.num_programs(1) + i]
    n_full = i - js
    q_t = q_ref[...]
    m_scr[...] = jnp.full(m_scr.shape, -jnp.inf, F32)
    acc_scr[...] = jnp.zeros(acc_scr.shape, F32)

    def scores(j, dst):
        dst[...] = jnp.dot(k_ref[j], q_t, preferred_element_type=F32)

    def consume(j, src, masked):
        s_t = src[...]
        if masked:
            key = lax.broadcasted_iota(jnp.int32, s_t.shape, 0)
            qry = lax.broadcasted_iota(jnp.int32, s_t.shape, 1)
            s_t = jnp.where(key <= qry, s_t, -jnp.inf)
        m_old = m_scr[...]
        m_new = jnp.maximum(m_old, jnp.max(s_t, axis=0, keepdims=True))
        alpha = jnp.exp2(m_old - m_new)
        p_t = jnp.exp2(s_t - m_new).astype(BF16)
        acc_scr[...] = alpha * acc_scr[...] + jnp.dot(v_ref[j], p_t, preferred_element_type=F32)
        m_scr[...] = m_new

    scores(js, sa_scr)

    def pair(jj, c):
        t0 = js + 2 * jj
        scores(t0 + 1, sb_scr)
        consume(t0, sa_scr, False)
        scores(t0 + 2, sa_scr)
        consume(t0 + 1, sb_scr, False)
        return c

    lax.fori_loop(0, lax.shift_right_logical(n_full, 1), pair, 0)

    @pl.when((n_full & 1) == 0)
    def _():
        consume(i, sa_scr, True)

    @pl.when((n_full & 1) == 1)
    def _():
        scores(i, sb_scr)
        consume(i - 1, sa_scr, False)
        consume(i, sb_scr, True)

    acc = acc_scr[...]
    o_ref[...] = (acc[:hd] / acc[hd:hd + 1]).astype(o_ref.dtype)


ATTN_DEAD_LOG2 = 160.0


def _first_live_tile(q_t, kh, f_log2, tq):
    nh, _, l = q_t.shape
    nt = l // tq
    qn = jnp.sqrt(jnp.sum(jnp.square(q_t.astype(F32)), axis=1))
    kn = jnp.sqrt(jnp.sum(jnp.square(kh.astype(F32)), axis=2))
    qn_t = jnp.max(qn.reshape(nh, nt, tq), axis=2)
    kn_t = jnp.max(kn.reshape(nh, nt, tq), axis=2)
    k_max = jnp.max(kn_t, axis=1, keepdims=True)
    f = f_log2.T.reshape(nh, nt, tq)
    reach = 1.01 * qn_t * (k_max + kn_t) + f[:, :, 0]
    gap = reach[:, :, None] - f[:, None, :, tq - 1]
    below = jnp.arange(nt)[None, :] < jnp.arange(nt)[:, None]
    dead = ((gap < -ATTN_DEAD_LOG2) & below[None]).astype(jnp.int32)
    return jnp.sum(jnp.cumprod(dead, axis=2), axis=2).reshape(-1)


def _attn_prompt(q_aug, k_aug, v_aug, fcum, hd):
    nh, nt, _, tq = v_aug.shape
    l = nt * tq
    d = nh * hd
    js = _first_live_tile(q_aug[:, :hd, :], k_aug[:, :, :hd], fcum * LOG2E, tq)
    k_aug = k_aug.reshape(nh, nt, tq, V7X_LANES)
    o_t = pl.pallas_call(
        functools.partial(_attn_kernel, hd=hd),
        grid_spec=pltpu.PrefetchScalarGridSpec(
            num_scalar_prefetch=1,
            grid=(nh, nt),
            in_specs=[pl.BlockSpec((None, V7X_LANES, tq), lambda h, i, js_ref: (h, 0, i)),
                      pl.BlockSpec((None, nt, tq, V7X_LANES), lambda h, i, js_ref: (h, 0, 0, 0)),
                      pl.BlockSpec((None, nt, ATTN_VROWS, tq), lambda h, i, js_ref: (h, 0, 0, 0))],
            out_specs=pl.BlockSpec((None, hd, tq), lambda h, i, js_ref: (h, 0, i)),
            scratch_shapes=[pltpu.VMEM((tq, tq), F32), pltpu.VMEM((tq, tq), F32),
                            pltpu.VMEM((1, tq), F32), pltpu.VMEM((ATTN_VROWS, tq), F32)]),
        out_shape=jax.ShapeDtypeStruct((nh, hd, l), BF16),
        compiler_params=_cparams(("arbitrary", "arbitrary")),
        name="attn_prompt",
    )(js, q_aug, k_aug, v_aug)
    return o_t.transpose(2, 0, 1).reshape(l, d)


def _attn_paged_kernel(pt_ref, qb_ref, knew_ref, vnew_ref, lfnew_ref, *refs, pages_per_step, hd):
    del pt_ref
    pps = pages_per_step
    k_refs = refs[:pps]
    v_refs = refs[pps:2 * pps]
    lf_refs = refs[2 * pps:3 * pps]
    o_ref, m_scr, l_scr, f_scr, s_scr, p_scr, a_scr, acc_scr = refs[3 * pps:]
    nh = N_HEADS
    d, ps = acc_scr.shape
    step = pl.program_id(1)

    @pl.when(step == 0)
    def _():
        m_scr[...] = jnp.full(m_scr.shape, -jnp.inf, F32)
        l_scr[...] = jnp.zeros(l_scr.shape, F32)
        f_scr[...] = jnp.zeros(f_scr.shape, F32)
        acc_scr[...] = jnp.zeros(acc_scr.shape, F32)

    def update(k_rows, v_rows, fs, n_valid):
        n = len(fs)
        for r in range(n):
            for h in range(nh):
                rows = slice(h * hd, (h + 1) * hd)
                s_scr[h:h + 1, r * ps:(r + 1) * ps] = jnp.sum(k_rows(r, rows) * qb_ref[rows, :],
                                                             axis=0, keepdims=True)
        z = s_scr[:, :n * ps] - jnp.concatenate(fs, axis=1)
        if n_valid is not None:
            z = jnp.where(lax.broadcasted_iota(jnp.int32, z.shape, 1) % ps < n_valid, z, -jnp.inf)
        m_old = m_scr[...]
        m_new = jnp.maximum(m_old, jnp.max(z, axis=1, keepdims=True))
        alpha = jnp.exp(m_old - m_new)
        p = jnp.exp(z - m_new)
        l_scr[...] = alpha * l_scr[...] + jnp.sum(p, axis=1, keepdims=True)
        m_scr[...] = m_new
        p_scr[:, :n * ps] = p
        a_scr[...] = jnp.broadcast_to(alpha, (nh, ps))
        for h in range(nh):
            rows = slice(h * hd, (h + 1) * hd)
            acc = jnp.broadcast_to(a_scr[h:h + 1, :], (hd, ps)) * acc_scr[rows, :]
            for r in range(n):
                acc = acc + jnp.broadcast_to(p_scr[h:h + 1, r * ps:(r + 1) * ps], (hd, ps)) * v_rows(r, rows)
            acc_scr[rows, :] = acc

    tri = (lax.broadcasted_iota(jnp.int32, (ps, ps), 0)
           <= lax.broadcasted_iota(jnp.int32, (ps, ps), 1)).astype(F32)
    fs = []
    carry = f_scr[...]
    for r in range(pps):
        fs.append(jnp.dot(lf_refs[r][...], tri, precision=HIGHEST, preferred_element_type=F32) + carry)
        carry = fs[-1][:, ps - 1:ps]
    update(lambda r, rows: k_refs[r][rows, :], lambda r, rows: v_refs[r][rows, :], fs, None)
    f_scr[...] = carry

    @pl.when(step == pl.num_programs(1) - 1)
    def _():
        f_new = jnp.broadcast_to(f_scr[...] + lfnew_ref[...], (nh, ps))
        update(lambda r, rows: jnp.broadcast_to(knew_ref[rows, :], (hd, ps)),
               lambda r, rows: jnp.broadcast_to(vnew_ref[rows, :], (hd, ps)), [f_new], 1)
        for h in range(nh):
            rows = slice(h * hd, (h + 1) * hd)
            o_h = jnp.sum(acc_scr[rows, :], axis=1, keepdims=True)
            o_ref[rows, :] = o_h / l_scr[h:h + 1, :]


def _attn_paged(q_bf16, k_new, v_new, lf_new, cache_k, cache_v, cache_logf, page_table,
                pages_per_step=8):
    b, d = k_new.shape
    nh = N_HEADS
    hd = d // nh
    n_pages = page_table.shape[1]
    n_phys, ps = cache_k.shape[0], cache_k.shape[1]
    pps = math.gcd(pages_per_step, n_pages)
    ck = cache_k.transpose(0, 2, 3, 1).reshape(n_phys, d, ps)
    cv = cache_v.transpose(0, 2, 3, 1).reshape(n_phys, d, ps)
    clf = cache_logf.transpose(0, 2, 1).astype(F32)
    qb = jnp.broadcast_to(q_bf16.astype(F32)[:, :, None], (b, d, ps))
    pt = page_table.reshape(-1).astype(jnp.int32)

    def page_map(r):
        return lambda s, g, pt_ref: (pt_ref[s * n_pages + g * pps + r], 0, 0)

    seq = lambda rows, width: pl.BlockSpec((None, rows, width), lambda s, g, pt_ref: (s, 0, 0))
    in_specs = ([seq(d, ps), seq(d, 1), seq(d, 1), seq(nh, 1)]
                + [pl.BlockSpec((None, d, ps), page_map(r)) for r in range(pps)]
                + [pl.BlockSpec((None, d, ps), page_map(r)) for r in range(pps)]
                + [pl.BlockSpec((None, nh, ps), page_map(r)) for r in range(pps)])
    out = pl.pallas_call(
        functools.partial(_attn_paged_kernel, pages_per_step=pps, hd=hd),
        grid_spec=pltpu.PrefetchScalarGridSpec(
            num_scalar_prefetch=1,
            grid=(b, n_pages // pps),
            in_specs=in_specs,
            out_specs=seq(d, 1),
            scratch_shapes=[pltpu.VMEM((nh, 1), F32), pltpu.VMEM((nh, 1), F32), pltpu.VMEM((nh, 1), F32),
                            pltpu.VMEM((nh, pps * ps), F32), pltpu.VMEM((nh, pps * ps), F32),
                            pltpu.VMEM((nh, ps), F32), pltpu.VMEM((d, ps), F32)]),
        out_shape=jax.ShapeDtypeStruct((b, d, 1), F32),
        compiler_params=_cparams(("arbitrary", "arbitrary")),
        name="attn_paged",
    )(pt, qb, k_new.reshape(b, d, 1), v_new.reshape(b, d, 1), lf_new.reshape(b, nh, 1),
      *([ck] * pps), *([cv] * pps), *([clf] * pps))
    return out.reshape(b, d).astype(BF16)


def _oproj_kernel(o_ref, h_ref, gate_ref, w_ref, out_ref):
    out_ref[...] = h_ref[...] + gate_ref[...] * jnp.dot(o_ref[...], w_ref[...],
                                                        preferred_element_type=F32)


def _oproj(o, h, gate, w_o, tm):
    m, d = h.shape
    return pl.pallas_call(
        _oproj_kernel,
        grid=(m // tm,),
        in_specs=[_row_spec(tm, d), _row_spec(tm, d), _mod_spec(gate, tm), _const_spec((d, d))],
        out_specs=_row_spec(tm, d),
        out_shape=jax.ShapeDtypeStruct((m, d), F32),
        compiler_params=_cparams(("arbitrary",)),
        name="oproj",
    )(o, h, gate, w_o)


def _row_tile(m):
    return min(m, 512)


def _layer_mods(mod_all, kv_all, rows, d):
    per_layer = [[mod_all[layer, rows, i * d:(i + 1) * d] for i in range(6)]
                 for layer in range(mod_all.shape[0])]
    kv = [kv_all[0, rows, i * d:(i + 1) * d] for i in range(2)]
    return per_layer, kv


def kernel(x_prompt, x_sample, c_prompt, c_sample, state_ssm_re, state_ssm_im, cache_k, cache_v, cache_logf, page_table, ada_w, ada_b, g_mix, g_mlp, mlp_w1, mlp_w2, ssm_lam_re, ssm_lam_im, ssm_log_dt, ssm_b_re, ssm_b_im, ssm_c_re, ssm_c_im, ssm_d, w_glu, kv_ada_w, kv_ada_b, g_kv, w_kvf, b_f, k_norm, w_q, q_norm, w_o):
    bp, l, d = x_prompt.shape
    bs = x_sample.shape[0]
    nh = N_HEADS
    hd = d // nh
    assert bp == 1 and x_sample.shape[1] == 1 and ada_w.shape[0] == 2
    assert l % (S5_SEGMENTS * S5_ROWS_PER_SEG) == 0 and S5_ROWS_PER_SEG % (2 * S5_CHUNK) == 0

    row = lambda a: a.reshape(1, -1).astype(F32)
    w_glu_b = w_glu[0].astype(BF16)
    w1_b = mlp_w1.astype(BF16)
    w2_b = mlp_w2.astype(BF16)
    w_q_b = w_q[0].astype(BF16)
    w_o_b = w_o[0].astype(BF16)
    w_kvf_b = jnp.concatenate(
        [w_kvf, jnp.zeros((d, V7X_LANES - nh), w_kvf.dtype)], axis=1).astype(BF16)
    b_f_p = jnp.concatenate([b_f.astype(F32), jnp.zeros((V7X_LANES - nh,), F32)]).reshape(1, -1)
    head_of = jnp.arange(d) // hd
    e = (head_of[:, None] == jnp.arange(V7X_LANES)[None, :]).astype(F32)
    e_mean = (e / hd).astype(BF16)
    e_t = e.T.astype(BF16)
    k_norm_full = jnp.tile(k_norm.astype(F32), nh).reshape(1, d)
    q_norm_full = jnp.tile(q_norm[0].astype(F32), nh).reshape(1, d)

    ab_re, ab_im, bb_re, bb_im = _s5_discretise(ssm_lam_re[0], ssm_lam_im[0], ssm_log_dt[0],
                                                ssm_b_re[0], ssm_b_im[0])
    n_local = l // S5_CHUNK // S5_SEGMENTS
    s5_w = _s5_chunk_weights(ab_re, ab_im, bb_re, bb_im, ssm_c_re[0], ssm_c_im[0], n_local)

    n_c = bp + bs
    c_rows = -(-n_c // 8) * 8
    c_all = jnp.concatenate([c_prompt, c_sample, jnp.zeros((c_rows - n_c, d), F32)], axis=0)
    mod_all = _mods(c_all, ada_w, ada_b.reshape(ada_b.shape[0], 1, -1))
    kv_all = _mods(c_all, kv_ada_w[None], kv_ada_b.reshape(1, 1, -1))

    def trunk(x, rows, s5_fn, qkv_attend):
        m = x.shape[0]
        tm = _row_tile(m)
        (l0, l1), (sh_kv, sc_kv) = _layer_mods(mod_all, kv_all, rows, d)
        sh_m, sc_m, g_m, sh_f, sc_f, g_f = l0
        h, st_re, st_im = s5_fn(x, row(g_mix[0]), sh_m, sc_m, g_m, tm)
        h = _mlp(h, row(g_mlp[0]), sh_f, sc_f, g_f, w1_b[0], w2_b[0], tm)
        sh_m, sc_m, g_m, sh_f, sc_f, g_f = l1
        qkv_args = (h, row(g_kv), sh_kv, sc_kv, row(g_mix[1]), sh_m, sc_m,
                    w_kvf_b, b_f_p, w_q_b, k_norm_full, q_norm_full, e_mean, e_t, tm)
        k, v, lf, o = qkv_attend(qkv_args)
        h = _oproj(o, h, g_m, w_o_b, tm)
        h = _mlp(h, row(g_mlp[1]), sh_f, sc_f, g_f, w1_b[1], w2_b[1], tm)
        return h, st_re, st_im, k, v, lf

    def s5_prompt(x, g, sh, sc, gate, tm):
        y, st_re, st_im = _s5_prompt(_ada_rms_chunked(x, g, sh, sc), s5_w)
        return _glu_chunked(x, y, g, row(ssm_d[0]), sh, sc, gate, w_glu_b), st_re, st_im

    def attend_prompt(qkv_args):
        k_t, v_t, lf_t, fcum, q_aug, k_aug, v_aug = _qkv_prompt(*qkv_args, LOG2E / math.sqrt(hd))
        o = _attn_prompt(q_aug, k_aug, v_aug, fcum, hd)
        tokens_last = lambda a: a.reshape(nh, hd, l).transpose(2, 0, 1)[None]
        return tokens_last(k_t), tokens_last(v_t), lf_t.T[None], o

    y_p, sre_p, sim_p, k_p, v_p, f_p = trunk(x_prompt[0], slice(0, 1), s5_prompt, attend_prompt)

    def s5_sample(x, g, sh, sc, gate, tm):
        u = _ada_rms(x, g, sh, sc, tm, F32)
        y, st_re, st_im = _s5_step(u, state_ssm_re[0], state_ssm_im[0], ab_re, ab_im, bb_re, bb_im,
                                   ssm_c_re[0], ssm_c_im[0])
        return _glu(x, y, g, row(ssm_d[0]), sh, sc, gate, w_glu_b, tm), st_re, st_im

    def attend_sample(qkv_args):
        k, v, lf, q = _qkv(*qkv_args, 1.0 / math.sqrt(hd))
        o = _attn_paged(q, k, v, lf, cache_k, cache_v, cache_logf, page_table)
        return k.reshape(bs, 1, nh, hd), v.reshape(bs, 1, nh, hd), lf[:, None], o

    y_s, sre_s, sim_s, k_s, v_s, f_s = trunk(x_sample[:, 0], slice(bp, bp + bs), s5_sample,
                                             attend_sample)

    sdt = state_ssm_re.dtype
    return (y_p[None], y_s[:, None],
            sre_p[None, None].astype(sdt), sim_p[None, None].astype(sdt), k_p, v_p, f_p,
            sre_s[None].astype(sdt), sim_s[None].astype(sdt), k_s, v_s, f_s)
```

```python
import functools
import math

import jax
import jax.numpy as jnp
from jax import lax
from jax.experimental import pallas as pl
from jax.experimental.pallas import tpu as pltpu

F32 = jnp.float32
BF16 = jnp.bfloat16
HIGHEST = lax.Precision.HIGHEST

EPS = 1e-6
GROUP_SIZE = 16
STATE_DIM = 64
N_HEADS = 16
PAGE_SIZE = 128
S5_CHUNK = 8
S5_SEGMENTS = 8
S5_ROWS_PER_SEG = 64
V7X_LANES = 128
V7X_VMEM_LIMIT = 56 * 1024 * 1024


def _cparams(sem):
    return pltpu.CompilerParams(dimension_semantics=sem, vmem_limit_bytes=V7X_VMEM_LIMIT)


def _const_spec(shape):
    nd = len(shape)
    return pl.BlockSpec(shape, lambda *_: (0,) * nd, pipeline_mode=pl.Buffered(1))


def _row_spec(tm, width):
    return pl.BlockSpec((tm, width), lambda i: (i, 0))


def _mod_spec(arr, tm):
    if arr.shape[0] == 1:
        return pl.BlockSpec((1, arr.shape[1]), lambda i: (0, 0))
    return pl.BlockSpec((tm, arr.shape[1]), lambda i: (i, 0))


def _rms_hat(x):
    return x * lax.rsqrt(jnp.mean(x * x, axis=-1, keepdims=True) + EPS)


def _mods_kernel(c_ref, w_ref, b_ref, o_ref):
    o_ref[...] = jnp.dot(c_ref[...], w_ref[...], precision=HIGHEST,
                         preferred_element_type=F32) + b_ref[...]


def _mods(c, w3, b3, tn=2048):
    nl, d, n = w3.shape
    mp = c.shape[0]
    return pl.pallas_call(
        _mods_kernel,
        grid=(nl, n // tn),
        in_specs=[pl.BlockSpec((mp, d), lambda l, j: (0, 0)),
                  pl.BlockSpec((None, d, tn), lambda l, j: (l, 0, j)),
                  pl.BlockSpec((None, 1, tn), lambda l, j: (l, 0, j))],
        out_specs=pl.BlockSpec((None, mp, tn), lambda l, j: (l, 0, j)),
        out_shape=jax.ShapeDtypeStruct((nl, mp, n), F32),
        compiler_params=_cparams(("arbitrary", "arbitrary")),
        name="mods",
    )(c, w3, b3)


def _ada_rms_kernel(x_ref, g_ref, sh_ref, sc_ref, o_ref):
    u = (_rms_hat(x_ref[...]) * g_ref[...]) * (1.0 + sc_ref[...]) + sh_ref[...]
    o_ref[...] = u.astype(o_ref.dtype)


def _ada_rms(x, g, sh, sc, tm, out_dtype):
    m, d = x.shape
    return pl.pallas_call(
        _ada_rms_kernel,
        grid=(m // tm,),
        in_specs=[_row_spec(tm, d), _const_spec((1, d)), _mod_spec(sh, tm), _mod_spec(sc, tm)],
        out_specs=_row_spec(tm, d),
        out_shape=jax.ShapeDtypeStruct((m, d), out_dtype),
        compiler_params=_cparams(("arbitrary",)),
        name="ada_rms",
    )(x, g, sh, sc)


def _chunk_rows(j, c, rows_per_seg):
    return pl.ds(c * S5_CHUNK + j, S5_SEGMENTS, stride=rows_per_seg)


def _ada_rms_chunked_kernel(x_ref, g_ref, sh_ref, sc_ref, o_ref, u_scr):
    seg, rps, d = x_ref.shape
    x = x_ref[...].reshape(seg * rps, d)
    u = (_rms_hat(x) * g_ref[...]) * (1.0 + sc_ref[...]) + sh_ref[...]
    for k in range(d // V7X_LANES):
        u_scr[k] = u[:, k * V7X_LANES:(k + 1) * V7X_LANES]
        for j in range(S5_CHUNK):
            cols = slice(j * V7X_LANES, (j + 1) * V7X_LANES)
            for c in range(0, rps // S5_CHUNK, 2):
                pair = jnp.concatenate([u_scr[k, _chunk_rows(j, c, rps), :],
                                        u_scr[k, _chunk_rows(j, c + 1, rps), :]], axis=0)
                o_ref[k, c * seg:(c + 2) * seg, cols] = pair.astype(o_ref.dtype)


def _ada_rms_chunked(x, g, sh, sc):
    l, d = x.shape
    seg, rps, t = S5_SEGMENTS, S5_ROWS_PER_SEG, S5_CHUNK
    nb = d // V7X_LANES
    return pl.pallas_call(
        _ada_rms_chunked_kernel,
        grid=(l // (seg * rps),),
        in_specs=[pl.BlockSpec((seg, rps, d), lambda i: (0, i, 0)), _const_spec((1, d)),
                  _const_spec((1, d)), _const_spec((1, d))],
        out_specs=pl.BlockSpec((nb, seg * rps // t, t * V7X_LANES), lambda i: (0, i, 0)),
        out_shape=jax.ShapeDtypeStruct((nb, l // t, t * V7X_LANES), BF16),
        scratch_shapes=[pltpu.VMEM((nb, seg * rps, V7X_LANES), F32)],
        compiler_params=_cparams(("arbitrary",)),
        name="ada_rms_chunked",
    )(x.reshape(seg, l // seg, d), g, sh, sc)


def _s5_discretise(lam_re, lam_im, log_dt, b_re, b_im):
    dt = jnp.exp(log_dt.astype(F32))[:, None]
    lr = lam_re.astype(F32)
    li = lam_im.astype(F32)
    mag = jnp.exp(lr * dt)
    ang = li * dt
    ab_re = mag * jnp.cos(ang)
    ab_im = mag * jnp.sin(ang)
    den = lr * lr + li * li
    zr = ((ab_re - 1) * lr + ab_im * li) / den
    zi = (ab_im * lr - (ab_re - 1) * li) / den
    br = b_re.astype(F32)
    bi = b_im.astype(F32)
    bb_re = zr[..., None] * br - zi[..., None] * bi
    bb_im = zr[..., None] * bi + zi[..., None] * br
    return ab_re, ab_im, bb_re, bb_im


def _cpow_table(ar, ai, n):
    tr = jnp.ones((1,) + ar.shape, F32)
    ti = jnp.zeros((1,) + ar.shape, F32)
    br, bi = ar, ai
    while tr.shape[0] < n:
        tr, ti = (jnp.concatenate([tr, tr * br - ti * bi], axis=0),
                  jnp.concatenate([ti, tr * bi + ti * br], axis=0))
        br, bi = br * br - bi * bi, 2.0 * br * bi
    return tr[:n], ti[:n]


def _block_lanes(x_re, x_im, nb):
    def blk(x):
        x = x.reshape((nb, x.shape[0] // nb) + x.shape[1:])
        x = jnp.moveaxis(x, 1, -2)
        return x.reshape(x.shape[:-2] + (x.shape[-2] * x.shape[-1],))
    return jnp.concatenate([blk(x_re), blk(x_im)], axis=-1)


_ein = functools.partial(jnp.einsum, precision=HIGHEST)


def _expand_groups(compact, row_w, col_w, gb):
    nb, r, c = compact.shape
    src = jnp.arange(c)
    dst = jnp.arange(c * gb)
    dst_src = (dst // (col_w * gb)) * col_w + dst % col_w
    rep = (src[:, None] == dst_src[None, :]).astype(BF16)
    wide = jnp.einsum('krc,cd->krd', compact.astype(BF16), rep, preferred_element_type=BF16)
    row_group = (jnp.arange(r) // row_w) % gb
    col_group = (dst // col_w) % gb
    return jnp.where(row_group[:, None] == col_group[None, :], wide, jnp.zeros((), BF16))


def _s5_chunk_weights(ab_re, ab_im, bb_re, bb_im, c_re, c_im, n_local):
    t = S5_CHUNK
    g, p = ab_re.shape
    h = GROUP_SIZE
    gb = V7X_LANES // h
    nb = g // gb
    cr = c_re.astype(F32)
    ci = c_im.astype(F32)
    pw_re, pw_im = _cpow_table(ab_re, ab_im, t + 1)
    cb_rr = _ein('gop,gph->gpoh', cr, bb_re) - _ein('gop,gph->gpoh', ci, bb_im)
    cb_ii = _ein('gop,gph->gpoh', cr, bb_im) + _ein('gop,gph->gpoh', ci, bb_re)
    pw_gt = lambda a: jnp.moveaxis(a[:t], 0, 1)[:, :, :, None, None]
    kern = jnp.sum(pw_gt(pw_re) * cb_rr[:, None] - pw_gt(pw_im) * cb_ii[:, None], axis=2)
    tau = jnp.arange(t)[None, :] - jnp.arange(t)[:, None]
    m = jnp.where((tau >= 0)[None, :, :, None, None], kern[:, jnp.clip(tau, 0, t - 1)], 0.0)
    m = m.reshape(nb, gb, t, t, h, h)
    m_all = _expand_groups(m.transpose(0, 2, 1, 5, 3, 4).reshape(nb, t * gb * h, t * h), h, h, gb)
    rev = t - 1 - jnp.arange(t)
    p_re = _ein('jgp,gph->gjhp', pw_re[rev], bb_re) - _ein('jgp,gph->gjhp', pw_im[rev], bb_im)
    p_im = _ein('jgp,gph->gjhp', pw_re[rev], bb_im) + _ein('jgp,gph->gjhp', pw_im[rev], bb_re)
    p_ri = jnp.stack([p_re, p_im], axis=1).reshape(nb, gb, 2, t, h, p)
    p_all = _expand_groups(p_ri.transpose(0, 3, 1, 4, 2, 5).reshape(nb, t * gb * h, 2 * p), h, p, gb)
    w_re = _ein('gop,tgp->gpto', cr, pw_re[1:]) - _ein('gop,tgp->gpto', ci, pw_im[1:])
    w_im = _ein('gop,tgp->gpto', cr, pw_im[1:]) + _ein('gop,tgp->gpto', ci, pw_re[1:])
    w_ri = jnp.stack([w_re, -w_im], axis=1).reshape(nb, gb, 2, p, t, h)
    q_all = _expand_groups(w_ri.transpose(0, 2, 1, 3, 4, 5).reshape(nb, 2 * gb * p, t * h), p, h, gb)
    at_re, at_im = pw_re[t], pw_im[t]
    lp_re, lp_im = _cpow_table(at_re, at_im, n_local + 1)
    apow = _block_lanes(jnp.moveaxis(lp_re[:n_local], 0, 1), jnp.moveaxis(lp_im[:n_local], 0, 1), nb)
    a_chunk = _block_lanes(at_re[:, None], at_im[:, None], nb)
    a_seg = _block_lanes(lp_re[n_local][:, None], lp_im[n_local][:, None], nb)
    return m_all.astype(BF16), p_all.astype(BF16), q_all.astype(BF16), a_chunk, apow, a_seg


def _s5_prompt_kernel(u_ref, p_ref, m_ref, q_ref, ac_ref, apow_ref, aseg_ref,
                      y_ref, fin_ref, sp_scr, *, n_local):
    seg = S5_SEGMENTS
    half = sp_scr.shape[1] // 2
    sp_scr[...] = jnp.dot(u_ref[...], p_ref[...], preferred_element_type=F32)
    ar = jnp.broadcast_to(ac_ref[:, :half], (seg, half))
    ai = jnp.broadcast_to(ac_ref[:, half:], (seg, half))

    def scan_step(k, carry):
        sr, si = carry
        rows = pl.ds(pl.multiple_of(k * seg, seg), seg)
        gr = sp_scr[rows, :half]
        gi = sp_scr[rows, half:]
        sp_scr[rows, :half] = sr
        sp_scr[rows, half:] = si
        return (ar * sr - ai * si + gr, ar * si + ai * sr + gi)

    zero = jnp.zeros((seg, half), F32)
    er, ei = lax.fori_loop(0, n_local, scan_step, (zero, zero))

    sgr = aseg_ref[:, :half]
    sgi = aseg_ref[:, half:]
    cr = jnp.zeros((1, half), F32)
    ci = jnp.zeros((1, half), F32)
    rows_r, rows_i = [], []
    for s in range(seg):
        rows_r.append(cr)
        rows_i.append(ci)
        cr, ci = (sgr * cr - sgi * ci + er[s:s + 1], sgr * ci + sgi * cr + ei[s:s + 1])
    fin_ref[...] = jnp.concatenate([cr, ci], axis=-1)
    sin_r = jnp.concatenate(rows_r, axis=0)
    sin_i = jnp.concatenate(rows_i, axis=0)

    def fix_step(k, c):
        rows = pl.ds(pl.multiple_of(k * seg, seg), seg)
        apk = apow_ref[pl.ds(k, 1), :]
        pr = jnp.broadcast_to(apk[:, :half], (seg, half))
        pi = jnp.broadcast_to(apk[:, half:], (seg, half))
        sp_scr[rows, :half] += pr * sin_r - pi * sin_i
        sp_scr[rows, half:] += pr * sin_i + pi * sin_r
        return c

    lax.fori_loop(0, n_local, fix_step, 0)
    y = (jnp.dot(sp_scr[...].astype(BF16), q_ref[...], preferred_element_type=F32)
         + jnp.dot(u_ref[...], m_ref[...], preferred_element_type=F32))
    y_ref[...] = y.astype(y_ref.dtype)


def _s5_prompt(u_chunked, weights):
    m_all, p_all, q_all, a_chunk, apow, a_seg = weights
    nb, nc, kw = u_chunked.shape
    n_local = nc // S5_SEGMENTS
    sw = p_all.shape[2]
    blk = lambda rows, cols: pl.BlockSpec((None, rows, cols), lambda k: (k, 0, 0))
    yt, fin = pl.pallas_call(
        functools.partial(_s5_prompt_kernel, n_local=n_local),
        grid=(nb,),
        in_specs=[blk(nc, kw), blk(kw, sw), blk(kw, kw), blk(sw, kw), blk(1, sw), blk(n_local, sw),
                  blk(1, sw)],
        out_specs=[blk(nc, kw), blk(1, sw)],
        out_shape=[jax.ShapeDtypeStruct((nb, nc, kw), BF16), jax.ShapeDtypeStruct((nb, 1, sw), F32)],
        scratch_shapes=[pltpu.VMEM((nc, sw), F32)],
        compiler_params=_cparams(("arbitrary",)),
        name="s5_prompt",
    )(u_chunked, p_all, m_all, q_all, a_chunk, apow, a_seg)
    fin = fin.reshape(nb, 2, -1, STATE_DIM)
    return yt, fin[:, 0].reshape(-1, STATE_DIM), fin[:, 1].reshape(-1, STATE_DIM)


def _s5_step_kernel(u_ref, h_ref, hs_ref, a1_ref, a2_ref, bbt_ref, cq_ref, y_ref, hn_ref):
    bu = jnp.einsum('gbh,ghp->gbp', u_ref[...], bbt_ref[...], precision=HIGHEST,
                    preferred_element_type=F32)
    hn = a1_ref[...] * h_ref[...] + a2_ref[...] * hs_ref[...] + bu
    hn_ref[...] = hn
    y_ref[...] = jnp.einsum('gbp,gph->gbh', hn, cq_ref[...], precision=HIGHEST,
                            preferred_element_type=F32)


def _s5_step(u, h0_re, h0_im, ab_re, ab_im, bb_re, bb_im, c_re, c_im):
    b, d = u.shape
    g, p = ab_re.shape
    h = GROUP_SIZE
    u3 = u.reshape(b, g, h).transpose(1, 0, 2)
    hc = jnp.concatenate([h0_re, h0_im], axis=-1).transpose(1, 0, 2).astype(F32)
    hs = jnp.concatenate([h0_im, h0_re], axis=-1).transpose(1, 0, 2).astype(F32)
    a1 = jnp.concatenate([ab_re, ab_re], axis=-1)[:, None, :]
    a2 = jnp.concatenate([-ab_im, ab_im], axis=-1)[:, None, :]
    bbt = jnp.concatenate([bb_re, bb_im], axis=1).transpose(0, 2, 1)
    cq = jnp.concatenate([c_re.astype(F32), -c_im.astype(F32)], axis=-1).transpose(0, 2, 1)
    y3, hn = pl.pallas_call(
        _s5_step_kernel,
        out_shape=[jax.ShapeDtypeStruct((g, b, h), F32), jax.ShapeDtypeStruct((g, b, 2 * p), F32)],
        compiler_params=pltpu.CompilerParams(vmem_limit_bytes=V7X_VMEM_LIMIT),
        name="s5_step",
    )(u3, hc, hs, a1, a2, bbt, cq)
    y = y3.transpose(1, 0, 2).reshape(b, d)
    hn = hn.transpose(1, 0, 2)
    return y, hn[..., :p], hn[..., p:]


def _glu_body(x, y_ssm, g_ref, d_ref, sh_ref, sc_ref, gate_ref, w_ref):
    dm = x.shape[1]
    u = (_rms_hat(x) * g_ref[...]) * (1.0 + sc_ref[...]) + sh_ref[...]
    y = y_ssm + d_ref[...] * u
    z = jnp.dot(jax.nn.gelu(y).astype(BF16), w_ref[...], preferred_element_type=F32)
    out = z[:, :dm] * jax.nn.sigmoid(z[:, dm:])
    return x + gate_ref[...] * out


def _glu_kernel(x_ref, y_ref, g_ref, d_ref, sh_ref, sc_ref, gate_ref, w_ref, o_ref):
    o_ref[...] = _glu_body(x_ref[...], y_ref[...].astype(F32), g_ref, d_ref, sh_ref, sc_ref,
                           gate_ref, w_ref)


def _glu(x, y_ssm, g, d_skip, sh, sc, gate, w_glu, tm):
    m, d = x.shape
    return pl.pallas_call(
        _glu_kernel,
        grid=(m // tm,),
        in_specs=[_row_spec(tm, d), _row_spec(tm, d), _const_spec((1, d)), _const_spec((1, d)),
                  _mod_spec(sh, tm), _mod_spec(sc, tm), _mod_spec(gate, tm), _const_spec((d, 2 * d))],
        out_specs=_row_spec(tm, d),
        out_shape=jax.ShapeDtypeStruct((m, d), F32),
        compiler_params=_cparams(("arbitrary",)),
        name="glu",
    )(x, y_ssm, g, d_skip, sh, sc, gate, w_glu)


def _glu_chunked_kernel(x_ref, y_ref, g_ref, d_ref, sh_ref, sc_ref, gate_ref, w_ref, o_ref, y_scr):
    seg, rps, d = x_ref.shape
    nb = d // V7X_LANES
    for k in range(nb):
        for j in range(S5_CHUNK):
            cols = slice(j * V7X_LANES, (j + 1) * V7X_LANES)
            for c in range(0, rps // S5_CHUNK, 2):
                pair = y_ref[k, c * seg:(c + 2) * seg, cols].astype(F32)
                y_scr[k, _chunk_rows(j, c, rps), :] = pair[:seg]
                y_scr[k, _chunk_rows(j, c + 1, rps), :] = pair[seg:]
    y_ssm = jnp.concatenate([y_scr[k] for k in range(nb)], axis=1)
    out = _glu_body(x_ref[...].reshape(seg * rps, d), y_ssm, g_ref, d_ref, sh_ref, sc_ref,
                    gate_ref, w_ref)
    o_ref[...] = out.reshape(seg, rps, d)


def _glu_chunked(x, y_chunked, g, d_skip, sh, sc, gate, w_glu):
    l, d = x.shape
    seg, rps, t = S5_SEGMENTS, S5_ROWS_PER_SEG, S5_CHUNK
    nb = d // V7X_LANES
    x_spec = pl.BlockSpec((seg, rps, d), lambda i: (0, i, 0))
    out = pl.pallas_call(
        _glu_chunked_kernel,
        grid=(l // (seg * rps),),
        in_specs=[x_spec, pl.BlockSpec((nb, seg * rps // t, t * V7X_LANES), lambda i: (0, i, 0)),
                  _const_spec((1, d)), _const_spec((1, d)), _const_spec((1, d)), _const_spec((1, d)),
                  _const_spec((1, d)), _const_spec((d, 2 * d))],
        out_specs=x_spec,
        out_shape=jax.ShapeDtypeStruct((seg, l // seg, d), F32),
        scratch_shapes=[pltpu.VMEM((nb, seg * rps, V7X_LANES), F32)],
        compiler_params=_cparams(("arbitrary",)),
        name="glu_chunked",
    )(x.reshape(seg, l // seg, d), y_chunked, g, d_skip, sh, sc, gate, w_glu)
    return out.reshape(l, d)


def _mlp_kernel(x_ref, g_ref, sh_ref, sc_ref, gate_ref, w1_ref, w2_ref, o_ref, *, ff_chunk):
    x = x_ref[...]
    xn = ((_rms_hat(x) * g_ref[...]) * (1.0 + sc_ref[...]) + sh_ref[...]).astype(BF16)
    acc = jnp.zeros(x.shape, F32)
    for c in range(w1_ref.shape[1] // ff_chunk):
        a = jnp.dot(xn, w1_ref[:, c * ff_chunk:(c + 1) * ff_chunk], preferred_element_type=F32)
        a = jnp.square(jnp.maximum(a, 0.0)).astype(BF16)
        acc = acc + jnp.dot(a, w2_ref[c * ff_chunk:(c + 1) * ff_chunk, :], preferred_element_type=F32)
    o_ref[...] = x + gate_ref[...] * acc


def _mlp(x, g, sh, sc, gate, w1, w2, tm, ff_chunk=1024):
    m, d = x.shape
    dff = w1.shape[1]
    return pl.pallas_call(
        functools.partial(_mlp_kernel, ff_chunk=ff_chunk),
        grid=(m // tm,),
        in_specs=[_row_spec(tm, d), _const_spec((1, d)), _mod_spec(sh, tm), _mod_spec(sc, tm),
                  _mod_spec(gate, tm), _const_spec((d, dff)), _const_spec((dff, d))],
        out_specs=_row_spec(tm, d),
        out_shape=jax.ShapeDtypeStruct((m, d), F32),
        compiler_params=_cparams(("arbitrary",)),
        name="mlp",
    )(x, g, sh, sc, gate, w1, w2)


def _head_norm(t, gain, e_ref, et_ref):
    msq = jnp.dot((t * t).astype(BF16), e_ref[...], preferred_element_type=F32)
    r = lax.rsqrt(msq + EPS)
    r_hi = r.astype(BF16)
    r_lo = (r - r_hi.astype(F32)).astype(BF16)
    r_full = (jnp.dot(r_hi, et_ref[...], preferred_element_type=F32)
              + jnp.dot(r_lo, et_ref[...], preferred_element_type=F32))
    return (t * r_full) * gain


def _qkv_kernel(h_ref, gkv_ref, shkv_ref, sckv_ref, gq_ref, shq_ref, scq_ref,
                wkvf_ref, bf_ref, wq_ref, kn_ref, qn_ref, e_ref, et_ref,
                k_ref, v_ref, lf_ref, q_ref, *, q_scale):
    x = h_ref[...]
    dm = x.shape[1]
    xh = _rms_hat(x)
    xkv = ((xh * gkv_ref[...]) * (1.0 + sckv_ref[...]) + shkv_ref[...]).astype(BF16)
    xq = ((xh * gq_ref[...]) * (1.0 + scq_ref[...]) + shq_ref[...]).astype(BF16)
    proj = jnp.dot(xkv, wkvf_ref[...], preferred_element_type=F32)
    k_ref[...] = _head_norm(proj[:, :dm], kn_ref[...], e_ref, et_ref)
    v_ref[...] = proj[:, dm:2 * dm]
    lf = jax.nn.log_sigmoid(proj[:, 2 * dm:] + bf_ref[...])
    lf_ref[...] = lf[:, :lf_ref.shape[1]]
    q = _head_norm(jnp.dot(xq, wq_ref[...], preferred_element_type=F32), qn_ref[...], e_ref, et_ref)
    q_ref[...] = (q * q_scale).astype(q_ref.dtype)


def _trunc_bf16(x):
    bits = lax.bitcast_convert_type(x, jnp.uint32) & jnp.uint32(0xFFFF0000)
    return lax.bitcast_convert_type(bits, F32)


def _qkv_prompt_kernel(h_ref, gkv_ref, shkv_ref, sckv_ref, gq_ref, shq_ref, scq_ref,
                       wkvf_ref, bf_ref, wq_ref, kn_ref, qn_ref, e_ref, et_ref,
                       kt_ref, vt_ref, lft_ref, f_ref, qa_ref, ka_ref, va_ref, carry_scr, *, q_scale):
    @pl.when(pl.program_id(0) == 0)
    def _():
        carry_scr[...] = jnp.zeros_like(carry_scr)

    x = h_ref[...]
    tm, dm = x.shape
    nh = N_HEADS
    hd = dm // nh
    xh = _rms_hat(x)
    xkv = ((xh * gkv_ref[...]) * (1.0 + sckv_ref[...]) + shkv_ref[...]).astype(BF16)
    xq = ((xh * gq_ref[...]) * (1.0 + scq_ref[...]) + shq_ref[...]).astype(BF16)
    proj = jnp.dot(xkv, wkvf_ref[...], preferred_element_type=F32)
    k = _head_norm(proj[:, :dm], kn_ref[...], e_ref, et_ref)
    v_t = proj[:, dm:2 * dm].T
    lf = jax.nn.log_sigmoid(proj[:, 2 * dm:] + bf_ref[...])
    q = _head_norm(jnp.dot(xq, wq_ref[...], preferred_element_type=F32), qn_ref[...], e_ref, et_ref)
    q_t = (q * q_scale).T
    kt_ref[...] = k.T
    vt_ref[...] = v_t
    lft_ref[...] = lf.T[:nh]
    tri = (lax.broadcasted_iota(jnp.int32, (tm, tm), 1)
           <= lax.broadcasted_iota(jnp.int32, (tm, tm), 0)).astype(F32)
    f = jnp.dot(tri, lf, precision=HIGHEST, preferred_element_type=F32) + carry_scr[...]
    carry_scr[...] = f[tm - 1:tm, :]
    f_ref[...] = f[:, :nh]

    q_pad = (lax.broadcasted_iota(jnp.int32, (V7X_LANES - hd, tm), 0) < ATTN_AUG).astype(BF16)
    v_pad = (lax.broadcasted_iota(jnp.int32, (ATTN_VROWS - hd, tm), 0) < 1).astype(BF16)
    lane = lax.broadcasted_iota(jnp.int32, (tm, V7X_LANES), 1)
    fs = f * (-LOG2E)
    for h in range(nh):
        rows = slice(h * hd, (h + 1) * hd)
        qa_ref[h, :hd, :] = q_t[rows].astype(BF16)
        qa_ref[h, hd:, :] = q_pad
        va_ref[h, :hd, :] = v_t[rows].astype(BF16)
        va_ref[h, hd:, :] = v_pad
        two = (2 * hd) * (h // 2)
        base = k[:, two:two + 2 * hd]
        if h % 2:
            base = pltpu.roll(base, hd, axis=1)
        hi = _trunc_bf16(jnp.broadcast_to(fs[:, h:h + 1], (tm, V7X_LANES)))
        rest = jnp.broadcast_to(fs[:, h:h + 1], (tm, V7X_LANES)) - hi
        mid = _trunc_bf16(rest)
        lo = _trunc_bf16(rest - mid)
        aug = jnp.where(lane == hd, hi, jnp.where(lane == hd + 1, mid, jnp.where(lane == hd + 2, lo, 0.0)))
        ka_ref[h] = jnp.where(lane < hd, base, aug).astype(BF16)


def _qkv_prompt(h, g_kv, sh_kv, sc_kv, g_q, sh_q, sc_q, w_kvf, b_f, w_q, k_norm, q_norm, e, et, tm,
                q_scale):
    l, d = h.shape
    nh = N_HEADS
    nkvf = w_kvf.shape[1]
    nt = l // tm
    col = lambda rows: pl.BlockSpec((rows, tm), lambda i: (0, i))
    return pl.pallas_call(
        functools.partial(_qkv_prompt_kernel, q_scale=q_scale),
        grid=(nt,),
        in_specs=[_row_spec(tm, d),
                  _const_spec((1, d)), _const_spec((1, d)), _const_spec((1, d)),
                  _const_spec((1, d)), _const_spec((1, d)), _const_spec((1, d)),
                  _const_spec((d, nkvf)), _const_spec((1, V7X_LANES)), _const_spec((d, d)),
                  _const_spec((1, d)), _const_spec((1, d)),
                  _const_spec((d, V7X_LANES)), _const_spec((V7X_LANES, d))],
        out_specs=[col(d), col(d), col(nh), _row_spec(tm, nh),
                   pl.BlockSpec((nh, V7X_LANES, tm), lambda i: (0, 0, i)),
                   pl.BlockSpec((nh, tm, V7X_LANES), lambda i: (0, i, 0)),
                   pl.BlockSpec((nh, None, ATTN_VROWS, tm), lambda i: (0, i, 0, 0))],
        out_shape=[jax.ShapeDtypeStruct((d, l), F32), jax.ShapeDtypeStruct((d, l), F32),
                   jax.ShapeDtypeStruct((nh, l), F32), jax.ShapeDtypeStruct((l, nh), F32),
                   jax.ShapeDtypeStruct((nh, V7X_LANES, l), BF16),
                   jax.ShapeDtypeStruct((nh, l, V7X_LANES), BF16),
                   jax.ShapeDtypeStruct((nh, nt, ATTN_VROWS, tm), BF16)],
        scratch_shapes=[pltpu.VMEM((1, V7X_LANES), F32)],
        compiler_params=_cparams(("arbitrary",)),
        name="qkv_prompt",
    )(h, g_kv, sh_kv, sc_kv, g_q, sh_q, sc_q, w_kvf, b_f, w_q, k_norm, q_norm, e, et)


def _qkv(h, g_kv, sh_kv, sc_kv, g_q, sh_q, sc_q, w_kvf, b_f, w_q, k_norm, q_norm, e, et, tm,
         q_scale):
    m, d = h.shape
    nkvf = w_kvf.shape[1]
    return pl.pallas_call(
        functools.partial(_qkv_kernel, q_scale=q_scale),
        grid=(m // tm,),
        in_specs=[_row_spec(tm, d),
                  _const_spec((1, d)), _mod_spec(sh_kv, tm), _mod_spec(sc_kv, tm),
                  _const_spec((1, d)), _mod_spec(sh_q, tm), _mod_spec(sc_q, tm),
                  _const_spec((d, nkvf)), _const_spec((1, V7X_LANES)), _const_spec((d, d)),
                  _const_spec((1, d)), _const_spec((1, d)),
                  _const_spec((d, V7X_LANES)), _const_spec((V7X_LANES, d))],
        out_specs=[_row_spec(tm, d), _row_spec(tm, d), _row_spec(tm, N_HEADS), _row_spec(tm, d)],
        out_shape=[jax.ShapeDtypeStruct((m, d), F32), jax.ShapeDtypeStruct((m, d), F32),
                   jax.ShapeDtypeStruct((m, N_HEADS), F32), jax.ShapeDtypeStruct((m, d), BF16)],
        compiler_params=_cparams(("arbitrary",)),
        name="qkv",
    )(h, g_kv, sh_kv, sc_kv, g_q, sh_q, sc_q, w_kvf, b_f, w_q, k_norm, q_norm, e, et)


LOG2E = 1.4426950408889634
ATTN_AUG = 3
ATTN_VROWS = 80
ATTN_QCOLS = 256


def _attn_kernel(js_ref, q_ref, k_ref, v_ref, o_ref, sa_scr, sb_scr, m_scr, acc_scr, *, hd):
    i = pl.program_id(1)
    js = js_ref[pl.program_id(0) * pl.num_programs(1) + i]
    n_full = i - js
    q_t = q_ref[...]
    m_scr[...] = jnp.full(m_scr.shape, -jnp.inf, F32)
    acc_scr[...] = jnp.zeros(acc_scr.shape, F32)

    def scores(j, dst):
        dst[...] = jnp.dot(k_ref[j], q_t, preferred_element_type=F32)

    def consume(j, src, masked):
        tq = src.shape[1]
        for c0 in range(0, tq, ATTN_QCOLS):
            cols = slice(c0, c0 + ATTN_QCOLS)
            s_t = src[:, cols]
            if masked:
                key = lax.broadcasted_iota(jnp.int32, s_t.shape, 0)
                qry = lax.broadcasted_iota(jnp.int32, s_t.shape, 1) + c0
                s_t = jnp.where(key <= qry, s_t, -jnp.inf)
            m_old = m_scr[:, cols]
            m_new = jnp.maximum(m_old, jnp.max(s_t, axis=0, keepdims=True))
            alpha = jnp.exp2(m_old - m_new)
            p_t = jnp.exp2(s_t - m_new).astype(BF16)
            acc_scr[:, cols] = alpha * acc_scr[:, cols] + jnp.dot(v_ref[j], p_t,
                                                                  preferred_element_type=F32)
            m_scr[:, cols] = m_new

    scores(js, sa_scr)

    def pair(jj, c):
        t0 = js + 2 * jj
        scores(t0 + 1, sb_scr)
        consume(t0, sa_scr, False)
        scores(t0 + 2, sa_scr)
        consume(t0 + 1, sb_scr, False)
        return c

    lax.fori_loop(0, lax.shift_right_logical(n_full, 1), pair, 0)

    @pl.when((n_full & 1) == 0)
    def _():
        consume(i, sa_scr, True)

    @pl.when((n_full & 1) == 1)
    def _():
        scores(i, sb_scr)
        consume(i - 1, sa_scr, False)
        consume(i, sb_scr, True)

    acc = acc_scr[...]
    o_ref[...] = (acc[:hd] / acc[hd:hd + 1]).astype(o_ref.dtype)


ATTN_DEAD_LOG2 = 152.0


def _first_live_tile(q_t, kh, f_log2, tq):
    nh, _, l = q_t.shape
    nt = l // tq
    qn = jnp.sqrt(jnp.sum(jnp.square(q_t.astype(F32)), axis=1))
    kn = jnp.sqrt(jnp.sum(jnp.square(kh.astype(F32)), axis=2))
    qn_t = jnp.max(qn.reshape(nh, nt, tq), axis=2)
    kn_t = jnp.max(kn.reshape(nh, nt, tq), axis=2)
    k_max = jnp.max(kn_t, axis=1, keepdims=True)
    f = f_log2.T.reshape(nh, nt, tq)
    reach = 1.01 * qn_t * (k_max + kn_t) + f[:, :, 0]
    gap = reach[:, :, None] - f[:, None, :, tq - 1]
    below = jnp.arange(nt)[None, :] < jnp.arange(nt)[:, None]
    dead = ((gap < -ATTN_DEAD_LOG2) & below[None]).astype(jnp.int32)
    return jnp.sum(jnp.cumprod(dead, axis=2), axis=2).reshape(-1)


def _attn_prompt(q_aug, k_aug, v_aug, fcum, hd):
    nh, nt, _, tq = v_aug.shape
    l = nt * tq
    d = nh * hd
    js = _first_live_tile(q_aug[:, :hd, :], k_aug[:, :, :hd], fcum * LOG2E, tq)
    k_aug = k_aug.reshape(nh, nt, tq, V7X_LANES)
    o_t = pl.pallas_call(
        functools.partial(_attn_kernel, hd=hd),
        grid_spec=pltpu.PrefetchScalarGridSpec(
            num_scalar_prefetch=1,
            grid=(nh, nt),
            in_specs=[pl.BlockSpec((None, V7X_LANES, tq), lambda h, i, js_ref: (h, 0, i)),
                      pl.BlockSpec((None, nt, tq, V7X_LANES), lambda h, i, js_ref: (h, 0, 0, 0)),
                      pl.BlockSpec((None, nt, ATTN_VROWS, tq), lambda h, i, js_ref: (h, 0, 0, 0))],
            out_specs=pl.BlockSpec((None, hd, tq), lambda h, i, js_ref: (h, 0, i)),
            scratch_shapes=[pltpu.VMEM((tq, tq), F32), pltpu.VMEM((tq, tq), F32),
                            pltpu.VMEM((1, tq), F32), pltpu.VMEM((ATTN_VROWS, tq), F32)]),
        out_shape=jax.ShapeDtypeStruct((nh, hd, l), BF16),
        compiler_params=_cparams(("arbitrary", "arbitrary")),
        name="attn_prompt",
    )(js, q_aug, k_aug, v_aug)
    return o_t.transpose(2, 0, 1).reshape(l, d)


def _attn_paged_kernel(pt_ref, qb_ref, knew_ref, vnew_ref, lfnew_ref, *refs, pages_per_step, hd):
    del pt_ref
    pps = pages_per_step
    k_refs = refs[:pps]
    v_refs = refs[pps:2 * pps]
    lf_refs = refs[2 * pps:3 * pps]
    o_ref, m_scr, l_scr, f_scr, s_scr, p_scr, a_scr, acc_scr = refs[3 * pps:]
    nh = N_HEADS
    d, ps = acc_scr.shape
    step = pl.program_id(1)

    @pl.when(step == 0)
    def _():
        m_scr[...] = jnp.full(m_scr.shape, -jnp.inf, F32)
        l_scr[...] = jnp.zeros(l_scr.shape, F32)
        f_scr[...] = jnp.zeros(f_scr.shape, F32)
        acc_scr[...] = jnp.zeros(acc_scr.shape, F32)

    def update(k_rows, v_rows, fs, n_valid):
        n = len(fs)
        for r in range(n):
            for h in range(nh):
                rows = slice(h * hd, (h + 1) * hd)
                s_scr[h:h + 1, r * ps:(r + 1) * ps] = jnp.sum(k_rows(r, rows) * qb_ref[rows, :],
                                                             axis=0, keepdims=True)
        z = s_scr[:, :n * ps] - jnp.concatenate(fs, axis=1)
        if n_valid is not None:
            z = jnp.where(lax.broadcasted_iota(jnp.int32, z.shape, 1) % ps < n_valid, z, -jnp.inf)
        m_old = m_scr[...]
        m_new = jnp.maximum(m_old, jnp.max(z, axis=1, keepdims=True))
        alpha = jnp.exp(m_old - m_new)
        p = jnp.exp(z - m_new)
        l_scr[...] = alpha * l_scr[...] + jnp.sum(p, axis=1, keepdims=True)
        m_scr[...] = m_new
        p_scr[:, :n * ps] = p
        a_scr[...] = jnp.broadcast_to(alpha, (nh, ps))
        for h in range(nh):
            rows = slice(h * hd, (h + 1) * hd)
            acc = jnp.broadcast_to(a_scr[h:h + 1, :], (hd, ps)) * acc_scr[rows, :]
            for r in range(n):
                acc = acc + jnp.broadcast_to(p_scr[h:h + 1, r * ps:(r + 1) * ps], (hd, ps)) * v_rows(r, rows)
            acc_scr[rows, :] = acc

    tri = (lax.broadcasted_iota(jnp.int32, (ps, ps), 0)
           <= lax.broadcasted_iota(jnp.int32, (ps, ps), 1)).astype(F32)
    fs = []
    carry = f_scr[...]
    for r in range(pps):
        fs.append(jnp.dot(lf_refs[r][...], tri, precision=HIGHEST, preferred_element_type=F32) + carry)
        carry = fs[-1][:, ps - 1:ps]
    update(lambda r, rows: k_refs[r][rows, :], lambda r, rows: v_refs[r][rows, :], fs, None)
    f_scr[...] = carry

    @pl.when(step == pl.num_programs(1) - 1)
    def _():
        f_new = jnp.broadcast_to(f_scr[...] + lfnew_ref[...], (nh, ps))
        update(lambda r, rows: jnp.broadcast_to(knew_ref[rows, :], (hd, ps)),
               lambda r, rows: jnp.broadcast_to(vnew_ref[rows, :], (hd, ps)), [f_new], 1)
        for h in range(nh):
            rows = slice(h * hd, (h + 1) * hd)
            o_h = jnp.sum(acc_scr[rows, :], axis=1, keepdims=True)
            o_ref[rows, :] = o_h / l_scr[h:h + 1, :]


def _attn_paged(q_bf16, k_new, v_new, lf_new, cache_k, cache_v, cache_logf, page_table,
                pages_per_step=16):
    b, d = k_new.shape
    nh = N_HEADS
    hd = d // nh
    n_pages = page_table.shape[1]
    n_phys, ps = cache_k.shape[0], cache_k.shape[1]
    pps = math.gcd(pages_per_step, n_pages)
    ck = cache_k.transpose(0, 2, 3, 1).reshape(n_phys, d, ps)
    cv = cache_v.transpose(0, 2, 3, 1).reshape(n_phys, d, ps)
    clf = cache_logf.transpose(0, 2, 1).astype(F32)
    qb = jnp.broadcast_to(q_bf16.astype(F32)[:, :, None], (b, d, ps))
    pt = page_table.reshape(-1).astype(jnp.int32)

    def page_map(r):
        return lambda s, g, pt_ref: (pt_ref[s * n_pages + g * pps + r], 0, 0)

    seq = lambda rows, width: pl.BlockSpec((None, rows, width), lambda s, g, pt_ref: (s, 0, 0))
    in_specs = ([seq(d, ps), seq(d, 1), seq(d, 1), seq(nh, 1)]
                + [pl.BlockSpec((None, d, ps), page_map(r % pps)) for r in range(2 * pps)]
                + [pl.BlockSpec((None, nh, ps), page_map(r)) for r in range(pps)])
    out = pl.pallas_call(
        functools.partial(_attn_paged_kernel, pages_per_step=pps, hd=hd),
        grid_spec=pltpu.PrefetchScalarGridSpec(
            num_scalar_prefetch=1,
            grid=(b, n_pages // pps),
            in_specs=in_specs,
            out_specs=seq(d, 1),
            scratch_shapes=[pltpu.VMEM((nh, 1), F32), pltpu.VMEM((nh, 1), F32), pltpu.VMEM((nh, 1), F32),
                            pltpu.VMEM((nh, pps * ps), F32), pltpu.VMEM((nh, pps * ps), F32),
                            pltpu.VMEM((nh, ps), F32), pltpu.VMEM((d, ps), F32)]),
        out_shape=jax.ShapeDtypeStruct((b, d, 1), F32),
        compiler_params=_cparams(("arbitrary", "arbitrary")),
        name="attn_paged",
    )(pt, qb, k_new.reshape(b, d, 1), v_new.reshape(b, d, 1), lf_new.reshape(b, nh, 1),
      *([ck] * pps), *([cv] * pps), *([clf] * pps))
    return out.reshape(b, d).astype(BF16)


def _oproj_kernel(o_ref, h_ref, gate_ref, w_ref, out_ref):
    out_ref[...] = h_ref[...] + gate_ref[...] * jnp.dot(o_ref[...], w_ref[...],
                                                        preferred_element_type=F32)


def _oproj(o, h, gate, w_o, tm):
    m, d = h.shape
    return pl.pallas_call(
        _oproj_kernel,
        grid=(m // tm,),
        in_specs=[_row_spec(tm, d), _row_spec(tm, d), _mod_spec(gate, tm), _const_spec((d, d))],
        out_specs=_row_spec(tm, d),
        out_shape=jax.ShapeDtypeStruct((m, d), F32),
        compiler_params=_cparams(("arbitrary",)),
        name="oproj",
    )(o, h, gate, w_o)


def _row_tile(m):
    return min(m, 512)


def _layer_mods(mod_all, kv_all, rows, d):
    per_layer = [[mod_all[layer, rows, i * d:(i + 1) * d] for i in range(6)]
                 for layer in range(mod_all.shape[0])]
    kv = [kv_all[0, rows, i * d:(i + 1) * d] for i in range(2)]
    return per_layer, kv


def kernel(x_prompt, x_sample, c_prompt, c_sample, state_ssm_re, state_ssm_im, cache_k, cache_v, cache_logf, page_table, ada_w, ada_b, g_mix, g_mlp, mlp_w1, mlp_w2, ssm_lam_re, ssm_lam_im, ssm_log_dt, ssm_b_re, ssm_b_im, ssm_c_re, ssm_c_im, ssm_d, w_glu, kv_ada_w, kv_ada_b, g_kv, w_kvf, b_f, k_norm, w_q, q_norm, w_o):
    bp, l, d = x_prompt.shape
    bs = x_sample.shape[0]
    nh = N_HEADS
    hd = d // nh
    assert bp == 1 and x_sample.shape[1] == 1 and ada_w.shape[0] == 2
    assert l % (S5_SEGMENTS * S5_ROWS_PER_SEG) == 0 and S5_ROWS_PER_SEG % (2 * S5_CHUNK) == 0

    row = lambda a: a.reshape(1, -1).astype(F32)
    w_glu_b = w_glu[0].astype(BF16)
    w1_b = mlp_w1.astype(BF16)
    w2_b = mlp_w2.astype(BF16)
    w_q_b = w_q[0].astype(BF16)
    w_o_b = w_o[0].astype(BF16)
    w_kvf_b = jnp.concatenate(
        [w_kvf, jnp.zeros((d, V7X_LANES - nh), w_kvf.dtype)], axis=1).astype(BF16)
    b_f_p = jnp.concatenate([b_f.astype(F32), jnp.zeros((V7X_LANES - nh,), F32)]).reshape(1, -1)
    head_of = jnp.arange(d) // hd
    e = (head_of[:, None] == jnp.arange(V7X_LANES)[None, :]).astype(F32)
    e_mean = (e / hd).astype(BF16)
    e_t = e.T.astype(BF16)
    k_norm_full = jnp.tile(k_norm.astype(F32), nh).reshape(1, d)
    q_norm_full = jnp.tile(q_norm[0].astype(F32), nh).reshape(1, d)

    ab_re, ab_im, bb_re, bb_im = _s5_discretise(ssm_lam_re[0], ssm_lam_im[0], ssm_log_dt[0],
                                                ssm_b_re[0], ssm_b_im[0])
    n_local = l // S5_CHUNK // S5_SEGMENTS
    s5_w = _s5_chunk_weights(ab_re, ab_im, bb_re, bb_im, ssm_c_re[0], ssm_c_im[0], n_local)

    n_c = bp + bs
    c_rows = -(-n_c // 8) * 8
    c_all = jnp.concatenate([c_prompt, c_sample, jnp.zeros((c_rows - n_c, d), F32)], axis=0)
    mod_all = _mods(c_all, ada_w, ada_b.reshape(ada_b.shape[0], 1, -1))
    kv_all = _mods(c_all, kv_ada_w[None], kv_ada_b.reshape(1, 1, -1))

    def trunk(x, rows, s5_fn, qkv_attend):
        m = x.shape[0]
        tm = _row_tile(m)
        (l0, l1), (sh_kv, sc_kv) = _layer_mods(mod_all, kv_all, rows, d)
        sh_m, sc_m, g_m, sh_f, sc_f, g_f = l0
        h, st_re, st_im = s5_fn(x, row(g_mix[0]), sh_m, sc_m, g_m, tm)
        h = _mlp(h, row(g_mlp[0]), sh_f, sc_f, g_f, w1_b[0], w2_b[0], tm)
        sh_m, sc_m, g_m, sh_f, sc_f, g_f = l1
        qkv_args = (h, row(g_kv), sh_kv, sc_kv, row(g_mix[1]), sh_m, sc_m,
                    w_kvf_b, b_f_p, w_q_b, k_norm_full, q_norm_full, e_mean, e_t, tm)
        k, v, lf, o = qkv_attend(qkv_args)
        h = _oproj(o, h, g_m, w_o_b, tm)
        h = _mlp(h, row(g_mlp[1]), sh_f, sc_f, g_f, w1_b[1], w2_b[1], tm)
        return h, st_re, st_im, k, v, lf

    def s5_prompt(x, g, sh, sc, gate, tm):
        y, st_re, st_im = _s5_prompt(_ada_rms_chunked(x, g, sh, sc), s5_w)
        return _glu_chunked(x, y, g, row(ssm_d[0]), sh, sc, gate, w_glu_b), st_re, st_im

    def attend_prompt(qkv_args):
        k_t, v_t, lf_t, fcum, q_aug, k_aug, v_aug = _qkv_prompt(*qkv_args, LOG2E / math.sqrt(hd))
        o = _attn_prompt(q_aug, k_aug, v_aug, fcum, hd)
        tokens_last = lambda a: a.reshape(nh, hd, l).transpose(2, 0, 1)[None]
        return tokens_last(k_t), tokens_last(v_t), lf_t.T[None], o

    y_p, sre_p, sim_p, k_p, v_p, f_p = trunk(x_prompt[0], slice(0, 1), s5_prompt, attend_prompt)

    def s5_sample(x, g, sh, sc, gate, tm):
        u = _ada_rms(x, g, sh, sc, tm, F32)
        y, st_re, st_im = _s5_step(u, state_ssm_re[0], state_ssm_im[0], ab_re, ab_im, bb_re, bb_im,
                                   ssm_c_re[0], ssm_c_im[0])
        return _glu(x, y, g, row(ssm_d[0]), sh, sc, gate, w_glu_b, tm), st_re, st_im

    def attend_sample(qkv_args):
        k, v, lf, q = _qkv(*qkv_args, 1.0 / math.sqrt(hd))
        o = _attn_paged(q, k, v, lf, cache_k, cache_v, cache_logf, page_table)
        return k.reshape(bs, 1, nh, hd), v.reshape(bs, 1, nh, hd), lf[:, None], o

    y_s, sre_s, sim_s, k_s, v_s, f_s = trunk(x_sample[:, 0], slice(bp, bp + bs), s5_sample,
                                             attend_sample)

    sdt = state_ssm_re.dtype
    return (y_p[None], y_s[:, None],
            sre_p[None, None].astype(sdt), sim_p[None, None].astype(sdt), k_p, v_p, f_p,
            sre_s[None].astype(sdt), sim_s[None].astype(sdt), k_s, v_s, f_s)
```

```python
import functools
import math

import jax
import jax.numpy as jnp
from jax import lax
from jax.experimental import pallas as pl
from jax.experimental.pallas import tpu as pltpu

F32 = jnp.float32
BF16 = jnp.bfloat16
HIGHEST = lax.Precision.HIGHEST

EPS = 1e-6
GROUP_SIZE = 16
STATE_DIM = 64
N_HEADS = 16
PAGE_SIZE = 128
S5_CHUNK = 8
S5_SEGMENTS = 8
S5_ROWS_PER_SEG = 64
V7X_LANES = 128
V7X_VMEM_LIMIT = 56 * 1024 * 1024


def _cparams(sem):
    return pltpu.CompilerParams(dimension_semantics=sem, vmem_limit_bytes=V7X_VMEM_LIMIT)


def _const_spec(shape):
    nd = len(shape)
    return pl.BlockSpec(shape, lambda *_: (0,) * nd, pipeline_mode=pl.Buffered(1))


def _row_spec(tm, width):
    return pl.BlockSpec((tm, width), lambda i: (i, 0))


def _mod_spec(arr, tm):
    if arr.shape[0] == 1:
        return pl.BlockSpec((1, arr.shape[1]), lambda i: (0, 0))
    return pl.BlockSpec((tm, arr.shape[1]), lambda i: (i, 0))


def _rms_hat(x):
    return x * lax.rsqrt(jnp.mean(x * x, axis=-1, keepdims=True) + EPS)


def _mods_kernel(c_ref, w_ref, b_ref, o_ref):
    o_ref[...] = jnp.dot(c_ref[...], w_ref[...], precision=HIGHEST,
                         preferred_element_type=F32) + b_ref[...]


def _mods(c, w3, b3, tn=2048):
    nl, d, n = w3.shape
    mp = c.shape[0]
    return pl.pallas_call(
        _mods_kernel,
        grid=(nl, n // tn),
        in_specs=[pl.BlockSpec((mp, d), lambda l, j: (0, 0)),
                  pl.BlockSpec((None, d, tn), lambda l, j: (l, 0, j)),
                  pl.BlockSpec((None, 1, tn), lambda l, j: (l, 0, j))],
        out_specs=pl.BlockSpec((None, mp, tn), lambda l, j: (l, 0, j)),
        out_shape=jax.ShapeDtypeStruct((nl, mp, n), F32),
        compiler_params=_cparams(("arbitrary", "arbitrary")),
        name="mods",
    )(c, w3, b3)


def _ada_rms_kernel(x_ref, g_ref, sh_ref, sc_ref, o_ref):
    u = (_rms_hat(x_ref[...]) * g_ref[...]) * (1.0 + sc_ref[...]) + sh_ref[...]
    o_ref[...] = u.astype(o_ref.dtype)


def _ada_rms(x, g, sh, sc, tm, out_dtype):
    m, d = x.shape
    return pl.pallas_call(
        _ada_rms_kernel,
        grid=(m // tm,),
        in_specs=[_row_spec(tm, d), _const_spec((1, d)), _mod_spec(sh, tm), _mod_spec(sc, tm)],
        out_specs=_row_spec(tm, d),
        out_shape=jax.ShapeDtypeStruct((m, d), out_dtype),
        compiler_params=_cparams(("arbitrary",)),
        name="ada_rms",
    )(x, g, sh, sc)


def _chunk_rows(j, c, rows_per_seg):
    return pl.ds(c * S5_CHUNK + j, S5_SEGMENTS, stride=rows_per_seg)


def _ada_rms_chunked_kernel(x_ref, g_ref, sh_ref, sc_ref, o_ref, u_scr):
    seg, rps, d = x_ref.shape
    x = x_ref[...].reshape(seg * rps, d)
    u = (_rms_hat(x) * g_ref[...]) * (1.0 + sc_ref[...]) + sh_ref[...]
    for k in range(d // V7X_LANES):
        u_scr[k] = u[:, k * V7X_LANES:(k + 1) * V7X_LANES]
        for j in range(S5_CHUNK):
            cols = slice(j * V7X_LANES, (j + 1) * V7X_LANES)
            for c in range(0, rps // S5_CHUNK, 2):
                pair = jnp.concatenate([u_scr[k, _chunk_rows(j, c, rps), :],
                                        u_scr[k, _chunk_rows(j, c + 1, rps), :]], axis=0)
                o_ref[k, c * seg:(c + 2) * seg, cols] = pair.astype(o_ref.dtype)


def _ada_rms_chunked(x, g, sh, sc):
    l, d = x.shape
    seg, rps, t = S5_SEGMENTS, S5_ROWS_PER_SEG, S5_CHUNK
    nb = d // V7X_LANES
    return pl.pallas_call(
        _ada_rms_chunked_kernel,
        grid=(l // (seg * rps),),
        in_specs=[pl.BlockSpec((seg, rps, d), lambda i: (0, i, 0)), _const_spec((1, d)),
                  _const_spec((1, d)), _const_spec((1, d))],
        out_specs=pl.BlockSpec((nb, seg * rps // t, t * V7X_LANES), lambda i: (0, i, 0)),
        out_shape=jax.ShapeDtypeStruct((nb, l // t, t * V7X_LANES), BF16),
        scratch_shapes=[pltpu.VMEM((nb, seg * rps, V7X_LANES), F32)],
        compiler_params=_cparams(("arbitrary",)),
        name="ada_rms_chunked",
    )(x.reshape(seg, l // seg, d), g, sh, sc)


def _s5_discretise(lam_re, lam_im, log_dt, b_re, b_im):
    dt = jnp.exp(log_dt.astype(F32))[:, None]
    lr = lam_re.astype(F32)
    li = lam_im.astype(F32)
    mag = jnp.exp(lr * dt)
    ang = li * dt
    ab_re = mag * jnp.cos(ang)
    ab_im = mag * jnp.sin(ang)
    den = lr * lr + li * li
    zr = ((ab_re - 1) * lr + ab_im * li) / den
    zi = (ab_im * lr - (ab_re - 1) * li) / den
    br = b_re.astype(F32)
    bi = b_im.astype(F32)
    bb_re = zr[..., None] * br - zi[..., None] * bi
    bb_im = zr[..., None] * bi + zi[..., None] * br
    return ab_re, ab_im, bb_re, bb_im


def _cpow_table(ar, ai, n):
    tr = jnp.ones((1,) + ar.shape, F32)
    ti = jnp.zeros((1,) + ar.shape, F32)
    br, bi = ar, ai
    while tr.shape[0] < n:
        tr, ti = (jnp.concatenate([tr, tr * br - ti * bi], axis=0),
                  jnp.concatenate([ti, tr * bi + ti * br], axis=0))
        br, bi = br * br - bi * bi, 2.0 * br * bi
    return tr[:n], ti[:n]


def _block_lanes(x_re, x_im, nb):
    def blk(x):
        x = x.reshape((nb, x.shape[0] // nb) + x.shape[1:])
        x = jnp.moveaxis(x, 1, -2)
        return x.reshape(x.shape[:-2] + (x.shape[-2] * x.shape[-1],))
    return jnp.concatenate([blk(x_re), blk(x_im)], axis=-1)


_ein = functools.partial(jnp.einsum, precision=HIGHEST)


def _expand_groups(compact, row_w, col_w, gb):
    nb, r, c = compact.shape
    src = jnp.arange(c)
    dst = jnp.arange(c * gb)
    dst_src = (dst // (col_w * gb)) * col_w + dst % col_w
    rep = (src[:, None] == dst_src[None, :]).astype(BF16)
    wide = jnp.einsum('krc,cd->krd', compact.astype(BF16), rep, preferred_element_type=BF16)
    row_group = (jnp.arange(r) // row_w) % gb
    col_group = (dst // col_w) % gb
    return jnp.where(row_group[:, None] == col_group[None, :], wide, jnp.zeros((), BF16))


def _s5_chunk_weights(ab_re, ab_im, bb_re, bb_im, c_re, c_im, n_local):
    t = S5_CHUNK
    g, p = ab_re.shape
    h = GROUP_SIZE
    gb = V7X_LANES // h
    nb = g // gb
    cr = c_re.astype(F32)
    ci = c_im.astype(F32)
    pw_re, pw_im = _cpow_table(ab_re, ab_im, t + 1)
    cb_rr = _ein('gop,gph->gpoh', cr, bb_re) - _ein('gop,gph->gpoh', ci, bb_im)
    cb_ii = _ein('gop,gph->gpoh', cr, bb_im) + _ein('gop,gph->gpoh', ci, bb_re)
    pw_gt = lambda a: jnp.moveaxis(a[:t], 0, 1)[:, :, :, None, None]
    kern = jnp.sum(pw_gt(pw_re) * cb_rr[:, None] - pw_gt(pw_im) * cb_ii[:, None], axis=2)
    tau = jnp.arange(t)[None, :] - jnp.arange(t)[:, None]
    m = jnp.where((tau >= 0)[None, :, :, None, None], kern[:, jnp.clip(tau, 0, t - 1)], 0.0)
    m = m.reshape(nb, gb, t, t, h, h)
    m_all = _expand_groups(m.transpose(0, 2, 1, 5, 3, 4).reshape(nb, t * gb * h, t * h), h, h, gb)
    rev = t - 1 - jnp.arange(t)
    p_re = _ein('jgp,gph->gjhp', pw_re[rev], bb_re) - _ein('jgp,gph->gjhp', pw_im[rev], bb_im)
    p_im = _ein('jgp,gph->gjhp', pw_re[rev], bb_im) + _ein('jgp,gph->gjhp', pw_im[rev], bb_re)
    p_ri = jnp.stack([p_re, p_im], axis=1).reshape(nb, gb, 2, t, h, p)
    p_all = _expand_groups(p_ri.transpose(0, 3, 1, 4, 2, 5).reshape(nb, t * gb * h, 2 * p), h, p, gb)
    w_re = _ein('gop,tgp->gpto', cr, pw_re[1:]) - _ein('gop,tgp->gpto', ci, pw_im[1:])
    w_im = _ein('gop,tgp->gpto', cr, pw_im[1:]) + _ein('gop,tgp->gpto', ci, pw_re[1:])
    w_ri = jnp.stack([w_re, -w_im], axis=1).reshape(nb, gb, 2, p, t, h)
    q_all = _expand_groups(w_ri.transpose(0, 2, 1, 3, 4, 5).reshape(nb, 2 * gb * p, t * h), p, h, gb)
    at_re, at_im = pw_re[t], pw_im[t]
    lp_re, lp_im = _cpow_table(at_re, at_im, n_local + 1)
    apow = _block_lanes(jnp.moveaxis(lp_re[:n_local], 0, 1), jnp.moveaxis(lp_im[:n_local], 0, 1), nb)
    a_chunk = _block_lanes(at_re[:, None], at_im[:, None], nb)
    a_seg = _block_lanes(lp_re[n_local][:, None], lp_im[n_local][:, None], nb)
    return m_all.astype(BF16), p_all.astype(BF16), q_all.astype(BF16), a_chunk, apow, a_seg


def _s5_prompt_kernel(u_ref, p_ref, m_ref, q_ref, ac_ref, apow_ref, aseg_ref,
                      y_ref, fin_ref, sp_scr, *, n_local):
    seg = S5_SEGMENTS
    half = sp_scr.shape[1] // 2
    sp_scr[...] = jnp.dot(u_ref[...], p_ref[...], preferred_element_type=F32)
    ar = jnp.broadcast_to(ac_ref[:, :half], (seg, half))
    ai = jnp.broadcast_to(ac_ref[:, half:], (seg, half))

    def scan_step(k, carry):
        sr, si = carry
        rows = pl.ds(pl.multiple_of(k * seg, seg), seg)
        gr = sp_scr[rows, :half]
        gi = sp_scr[rows, half:]
        sp_scr[rows, :half] = sr
        sp_scr[rows, half:] = si
        return (ar * sr - ai * si + gr, ar * si + ai * sr + gi)

    zero = jnp.zeros((seg, half), F32)
    er, ei = lax.fori_loop(0, n_local, scan_step, (zero, zero))

    sgr = aseg_ref[:, :half]
    sgi = aseg_ref[:, half:]
    cr = jnp.zeros((1, half), F32)
    ci = jnp.zeros((1, half), F32)
    rows_r, rows_i = [], []
    for s in range(seg):
        rows_r.append(cr)
        rows_i.append(ci)
        cr, ci = (sgr * cr - sgi * ci + er[s:s + 1], sgr * ci + sgi * cr + ei[s:s + 1])
    fin_ref[...] = jnp.concatenate([cr, ci], axis=-1)
    sin_r = jnp.concatenate(rows_r, axis=0)
    sin_i = jnp.concatenate(rows_i, axis=0)

    def fix_step(k, c):
        rows = pl.ds(pl.multiple_of(k * seg, seg), seg)
        apk = apow_ref[pl.ds(k, 1), :]
        pr = jnp.broadcast_to(apk[:, :half], (seg, half))
        pi = jnp.broadcast_to(apk[:, half:], (seg, half))
        sp_scr[rows, :half] += pr * sin_r - pi * sin_i
        sp_scr[rows, half:] += pr * sin_i + pi * sin_r
        return c

    lax.fori_loop(0, n_local, fix_step, 0)
    y = (jnp.dot(sp_scr[...].astype(BF16), q_ref[...], preferred_element_type=F32)
         + jnp.dot(u_ref[...], m_ref[...], preferred_element_type=F32))
    y_ref[...] = y.astype(y_ref.dtype)


def _s5_prompt(u_chunked, weights):
    m_all, p_all, q_all, a_chunk, apow, a_seg = weights
    nb, nc, kw = u_chunked.shape
    n_local = nc // S5_SEGMENTS
    sw = p_all.shape[2]
    blk = lambda rows, cols: pl.BlockSpec((None, rows, cols), lambda k: (k, 0, 0))
    yt, fin = pl.pallas_call(
        functools.partial(_s5_prompt_kernel, n_local=n_local),
        grid=(nb,),
        in_specs=[blk(nc, kw), blk(kw, sw), blk(kw, kw), blk(sw, kw), blk(1, sw), blk(n_local, sw),
                  blk(1, sw)],
        out_specs=[blk(nc, kw), blk(1, sw)],
        out_shape=[jax.ShapeDtypeStruct((nb, nc, kw), BF16), jax.ShapeDtypeStruct((nb, 1, sw), F32)],
        scratch_shapes=[pltpu.VMEM((nc, sw), F32)],
        compiler_params=_cparams(("arbitrary",)),
        name="s5_prompt",
    )(u_chunked, p_all, m_all, q_all, a_chunk, apow, a_seg)
    fin = fin.reshape(nb, 2, -1, STATE_DIM)
    return yt, fin[:, 0].reshape(-1, STATE_DIM), fin[:, 1].reshape(-1, STATE_DIM)


def _s5_step_kernel(u_ref, h_ref, hs_ref, a1_ref, a2_ref, bbt_ref, cq_ref, y_ref, hn_ref):
    bu = jnp.einsum('gbh,ghp->gbp', u_ref[...], bbt_ref[...], precision=HIGHEST,
                    preferred_element_type=F32)
    hn = a1_ref[...] * h_ref[...] + a2_ref[...] * hs_ref[...] + bu
    hn_ref[...] = hn
    y_ref[...] = jnp.einsum('gbp,gph->gbh', hn, cq_ref[...], precision=HIGHEST,
                            preferred_element_type=F32)


def _s5_step(u, h0_re, h0_im, ab_re, ab_im, bb_re, bb_im, c_re, c_im):
    b, d = u.shape
    g, p = ab_re.shape
    h = GROUP_SIZE
    u3 = u.reshape(b, g, h).transpose(1, 0, 2)
    hc = jnp.concatenate([h0_re, h0_im], axis=-1).transpose(1, 0, 2).astype(F32)
    hs = jnp.concatenate([h0_im, h0_re], axis=-1).transpose(1, 0, 2).astype(F32)
    a1 = jnp.concatenate([ab_re, ab_re], axis=-1)[:, None, :]
    a2 = jnp.concatenate([-ab_im, ab_im], axis=-1)[:, None, :]
    bbt = jnp.concatenate([bb_re, bb_im], axis=1).transpose(0, 2, 1)
    cq = jnp.concatenate([c_re.astype(F32), -c_im.astype(F32)], axis=-1).transpose(0, 2, 1)
    y3, hn = pl.pallas_call(
        _s5_step_kernel,
        out_shape=[jax.ShapeDtypeStruct((g, b, h), F32), jax.ShapeDtypeStruct((g, b, 2 * p), F32)],
        compiler_params=pltpu.CompilerParams(vmem_limit_bytes=V7X_VMEM_LIMIT),
        name="s5_step",
    )(u3, hc, hs, a1, a2, bbt, cq)
    y = y3.transpose(1, 0, 2).reshape(b, d)
    hn = hn.transpose(1, 0, 2)
    return y, hn[..., :p], hn[..., p:]


def _glu_body(x, y_ssm, g_ref, d_ref, sh_ref, sc_ref, gate_ref, w_ref):
    dm = x.shape[1]
    u = (_rms_hat(x) * g_ref[...]) * (1.0 + sc_ref[...]) + sh_ref[...]
    y = y_ssm + d_ref[...] * u
    z = jnp.dot(jax.nn.gelu(y).astype(BF16), w_ref[...], preferred_element_type=F32)
    out = z[:, :dm] * jax.nn.sigmoid(z[:, dm:])
    return x + gate_ref[...] * out


def _glu_kernel(x_ref, y_ref, g_ref, d_ref, sh_ref, sc_ref, gate_ref, w_ref, o_ref):
    o_ref[...] = _glu_body(x_ref[...], y_ref[...].astype(F32), g_ref, d_ref, sh_ref, sc_ref,
                           gate_ref, w_ref)


def _glu(x, y_ssm, g, d_skip, sh, sc, gate, w_glu, tm):
    m, d = x.shape
    return pl.pallas_call(
        _glu_kernel,
        grid=(m // tm,),
        in_specs=[_row_spec(tm, d), _row_spec(tm, d), _const_spec((1, d)), _const_spec((1, d)),
                  _mod_spec(sh, tm), _mod_spec(sc, tm), _mod_spec(gate, tm), _const_spec((d, 2 * d))],
        out_specs=_row_spec(tm, d),
        out_shape=jax.ShapeDtypeStruct((m, d), F32),
        compiler_params=_cparams(("arbitrary",)),
        name="glu",
    )(x, y_ssm, g, d_skip, sh, sc, gate, w_glu)


def _glu_chunked_kernel(x_ref, y_ref, g_ref, d_ref, sh_ref, sc_ref, gate_ref, w_ref,
                        g2_ref, sh2_ref, sc2_ref, gate2_ref, w1_ref, w2_ref, o_ref, y_scr):
    seg, rps, d = x_ref.shape
    nb = d // V7X_LANES
    for k in range(nb):
        for j in range(S5_CHUNK):
            cols = slice(j * V7X_LANES, (j + 1) * V7X_LANES)
            for c in range(0, rps // S5_CHUNK, 2):
                pair = y_ref[k, c * seg:(c + 2) * seg, cols].astype(F32)
                y_scr[k, _chunk_rows(j, c, rps), :] = pair[:seg]
                y_scr[k, _chunk_rows(j, c + 1, rps), :] = pair[seg:]
    y_ssm = jnp.concatenate([y_scr[k] for k in range(nb)], axis=1)
    out = _glu_body(x_ref[...].reshape(seg * rps, d), y_ssm, g_ref, d_ref, sh_ref, sc_ref,
                    gate_ref, w_ref)
    out = _mlp_body(out, g2_ref, sh2_ref, sc2_ref, gate2_ref, w1_ref, w2_ref)
    o_ref[...] = out.reshape(seg, rps, d)


def _glu_mlp_chunked(x, y_chunked, g, d_skip, sh, sc, gate, w_glu, g2, sh2, sc2, gate2, w1, w2):
    l, d = x.shape
    dff = w1.shape[1]
    seg, rps, t = S5_SEGMENTS, S5_ROWS_PER_SEG, S5_CHUNK
    nb = d // V7X_LANES
    x_spec = pl.BlockSpec((seg, rps, d), lambda i: (0, i, 0))
    out = pl.pallas_call(
        _glu_chunked_kernel,
        grid=(l // (seg * rps),),
        in_specs=[x_spec, pl.BlockSpec((nb, seg * rps // t, t * V7X_LANES), lambda i: (0, i, 0)),
                  _const_spec((1, d)), _const_spec((1, d)), _const_spec((1, d)), _const_spec((1, d)),
                  _const_spec((1, d)), _const_spec((d, 2 * d)),
                  _const_spec((1, d)), _const_spec((1, d)), _const_spec((1, d)), _const_spec((1, d)),
                  _const_spec((d, dff)), _const_spec((dff, d))],
        out_specs=x_spec,
        out_shape=jax.ShapeDtypeStruct((seg, l // seg, d), F32),
        scratch_shapes=[pltpu.VMEM((nb, seg * rps, V7X_LANES), F32)],
        compiler_params=_cparams(("arbitrary",)),
        name="glu_mlp_chunked",
    )(x.reshape(seg, l // seg, d), y_chunked, g, d_skip, sh, sc, gate, w_glu, g2, sh2, sc2, gate2, w1, w2)
    return out.reshape(l, d)


MLP_FF_CHUNK = 1024


def _mlp_body(x, g_ref, sh_ref, sc_ref, gate_ref, w1_ref, w2_ref):
    xn = ((_rms_hat(x) * g_ref[...]) * (1.0 + sc_ref[...]) + sh_ref[...]).astype(BF16)
    acc = jnp.zeros(x.shape, F32)
    for c in range(0, w1_ref.shape[1], MLP_FF_CHUNK):
        a = jnp.dot(xn, w1_ref[:, c:c + MLP_FF_CHUNK], preferred_element_type=F32)
        a = jnp.square(jnp.maximum(a, 0.0)).astype(BF16)
        acc = acc + jnp.dot(a, w2_ref[c:c + MLP_FF_CHUNK, :], preferred_element_type=F32)
    return x + gate_ref[...] * acc


def _mlp_kernel(x_ref, g_ref, sh_ref, sc_ref, gate_ref, w1_ref, w2_ref, o_ref):
    o_ref[...] = _mlp_body(x_ref[...], g_ref, sh_ref, sc_ref, gate_ref, w1_ref, w2_ref)


def _oproj_mlp_kernel(o_ref, h_ref, gate_m_ref, wo_ref, g_ref, sh_ref, sc_ref, gate_ref, w1_ref, w2_ref,
                      out_ref):
    x = h_ref[...] + gate_m_ref[...] * jnp.dot(o_ref[...], wo_ref[...], preferred_element_type=F32)
    out_ref[...] = _mlp_body(x, g_ref, sh_ref, sc_ref, gate_ref, w1_ref, w2_ref)


def _oproj_mlp(o, h, gate_m, w_o, g, sh, sc, gate, w1, w2, tm):
    m, d = h.shape
    dff = w1.shape[1]
    return pl.pallas_call(
        _oproj_mlp_kernel,
        grid=(m // tm,),
        in_specs=[_row_spec(tm, d), _row_spec(tm, d), _mod_spec(gate_m, tm), _const_spec((d, d)),
                  _const_spec((1, d)), _mod_spec(sh, tm), _mod_spec(sc, tm), _mod_spec(gate, tm),
                  _const_spec((d, dff)), _const_spec((dff, d))],
        out_specs=_row_spec(tm, d),
        out_shape=jax.ShapeDtypeStruct((m, d), F32),
        compiler_params=_cparams(("arbitrary",)),
        name="oproj_mlp",
    )(o, h, gate_m, w_o, g, sh, sc, gate, w1, w2)


def _mlp(x, g, sh, sc, gate, w1, w2, tm):
    m, d = x.shape
    dff = w1.shape[1]
    return pl.pallas_call(
        _mlp_kernel,
        grid=(m // tm,),
        in_specs=[_row_spec(tm, d), _const_spec((1, d)), _mod_spec(sh, tm), _mod_spec(sc, tm),
                  _mod_spec(gate, tm), _const_spec((d, dff)), _const_spec((dff, d))],
        out_specs=_row_spec(tm, d),
        out_shape=jax.ShapeDtypeStruct((m, d), F32),
        compiler_params=_cparams(("arbitrary",)),
        name="mlp",
    )(x, g, sh, sc, gate, w1, w2)


def _head_norm(t, gain, e_ref, et_ref):
    msq = jnp.dot((t * t).astype(BF16), e_ref[...], preferred_element_type=F32)
    r = lax.rsqrt(msq + EPS)
    r_hi = r.astype(BF16)
    r_lo = (r - r_hi.astype(F32)).astype(BF16)
    r_full = (jnp.dot(r_hi, et_ref[...], preferred_element_type=F32)
              + jnp.dot(r_lo, et_ref[...], preferred_element_type=F32))
    return (t * r_full) * gain


def _qkv_kernel(h_ref, gkv_ref, shkv_ref, sckv_ref, gq_ref, shq_ref, scq_ref,
                wkvf_ref, bf_ref, wq_ref, kn_ref, qn_ref, e_ref, et_ref,
                k_ref, v_ref, lf_ref, q_ref, *, q_scale):
    x = h_ref[...]
    dm = x.shape[1]
    xh = _rms_hat(x)
    xkv = ((xh * gkv_ref[...]) * (1.0 + sckv_ref[...]) + shkv_ref[...]).astype(BF16)
    xq = ((xh * gq_ref[...]) * (1.0 + scq_ref[...]) + shq_ref[...]).astype(BF16)
    proj = jnp.dot(xkv, wkvf_ref[...], preferred_element_type=F32)
    k_ref[...] = _head_norm(proj[:, :dm], kn_ref[...], e_ref, et_ref)
    v_ref[...] = proj[:, dm:2 * dm]
    lf = jax.nn.log_sigmoid(proj[:, 2 * dm:] + bf_ref[...])
    lf_ref[...] = lf[:, :lf_ref.shape[1]]
    q = _head_norm(jnp.dot(xq, wq_ref[...], preferred_element_type=F32), qn_ref[...], e_ref, et_ref)
    q_ref[...] = (q * q_scale).astype(q_ref.dtype)


def _trunc_bf16(x):
    bits = lax.bitcast_convert_type(x, jnp.uint32) & jnp.uint32(0xFFFF0000)
    return lax.bitcast_convert_type(bits, F32)


def _qkv_prompt_kernel(h_ref, gkv_ref, shkv_ref, sckv_ref, gq_ref, shq_ref, scq_ref,
                       wkvf_ref, bf_ref, wq_ref, kn_ref, qn_ref, e_ref, et_ref,
                       kt_ref, vt_ref, lft_ref, f_ref, qa_ref, ka_ref, va_ref, carry_scr, *, q_scale):
    @pl.when(pl.program_id(0) == 0)
    def _():
        carry_scr[...] = jnp.zeros_like(carry_scr)

    x = h_ref[...]
    tm, dm = x.shape
    nh = N_HEADS
    hd = dm // nh
    xh = _rms_hat(x)
    xkv = ((xh * gkv_ref[...]) * (1.0 + sckv_ref[...]) + shkv_ref[...]).astype(BF16)
    xq = ((xh * gq_ref[...]) * (1.0 + scq_ref[...]) + shq_ref[...]).astype(BF16)
    proj = jnp.dot(xkv, wkvf_ref[...], preferred_element_type=F32)
    k = _head_norm(proj[:, :dm], kn_ref[...], e_ref, et_ref)
    v_t = proj[:, dm:2 * dm].T
    lf = jax.nn.log_sigmoid(proj[:, 2 * dm:] + bf_ref[...])
    q = _head_norm(jnp.dot(xq, wq_ref[...], preferred_element_type=F32), qn_ref[...], e_ref, et_ref)
    q_t = (q * q_scale).T
    kt_ref[...] = k.T
    vt_ref[...] = v_t
    lft_ref[...] = lf.T[:nh]
    tri = (lax.broadcasted_iota(jnp.int32, (tm, tm), 1)
           <= lax.broadcasted_iota(jnp.int32, (tm, tm), 0)).astype(F32)
    f = jnp.dot(tri, lf, precision=HIGHEST, preferred_element_type=F32) + carry_scr[...]
    carry_scr[...] = f[tm - 1:tm, :]
    f_ref[...] = f[:, :nh]

    q_pad = (lax.broadcasted_iota(jnp.int32, (V7X_LANES - hd, tm), 0) < ATTN_AUG).astype(BF16)
    v_pad = (lax.broadcasted_iota(jnp.int32, (ATTN_VROWS - hd, tm), 0) < 1).astype(BF16)
    lane = lax.broadcasted_iota(jnp.int32, (tm, V7X_LANES), 1)
    fs = f * (-LOG2E)
    for h in range(nh):
        rows = slice(h * hd, (h + 1) * hd)
        qa_ref[h, :hd, :] = q_t[rows].astype(BF16)
        qa_ref[h, hd:, :] = q_pad
        va_ref[h, :hd, :] = v_t[rows].astype(BF16)
        va_ref[h, hd:, :] = v_pad
        two = (2 * hd) * (h // 2)
        base = k[:, two:two + 2 * hd]
        if h % 2:
            base = pltpu.roll(base, hd, axis=1)
        hi = _trunc_bf16(jnp.broadcast_to(fs[:, h:h + 1], (tm, V7X_LANES)))
        rest = jnp.broadcast_to(fs[:, h:h + 1], (tm, V7X_LANES)) - hi
        mid = _trunc_bf16(rest)
        lo = _trunc_bf16(rest - mid)
        aug = jnp.where(lane == hd, hi, jnp.where(lane == hd + 1, mid, jnp.where(lane == hd + 2, lo, 0.0)))
        ka_ref[h] = jnp.where(lane < hd, base, aug).astype(BF16)


def _qkv_prompt(h, g_kv, sh_kv, sc_kv, g_q, sh_q, sc_q, w_kvf, b_f, w_q, k_norm, q_norm, e, et, tm,
                q_scale):
    l, d = h.shape
    nh = N_HEADS
    nkvf = w_kvf.shape[1]
    nt = l // tm
    col = lambda rows: pl.BlockSpec((rows, tm), lambda i: (0, i))
    return pl.pallas_call(
        functools.partial(_qkv_prompt_kernel, q_scale=q_scale),
        grid=(nt,),
        in_specs=[_row_spec(tm, d),
                  _const_spec((1, d)), _const_spec((1, d)), _const_spec((1, d)),
                  _const_spec((1, d)), _const_spec((1, d)), _const_spec((1, d)),
                  _const_spec((d, nkvf)), _const_spec((1, V7X_LANES)), _const_spec((d, d)),
                  _const_spec((1, d)), _const_spec((1, d)),
                  _const_spec((d, V7X_LANES)), _const_spec((V7X_LANES, d))],
        out_specs=[col(d), col(d), col(nh), _row_spec(tm, nh),
                   pl.BlockSpec((nh, V7X_LANES, tm), lambda i: (0, 0, i)),
                   pl.BlockSpec((nh, tm, V7X_LANES), lambda i: (0, i, 0)),
                   pl.BlockSpec((nh, None, ATTN_VROWS, tm), lambda i: (0, i, 0, 0))],
        out_shape=[jax.ShapeDtypeStruct((d, l), F32), jax.ShapeDtypeStruct((d, l), F32),
                   jax.ShapeDtypeStruct((nh, l), F32), jax.ShapeDtypeStruct((l, nh), F32),
                   jax.ShapeDtypeStruct((nh, V7X_LANES, l), BF16),
                   jax.ShapeDtypeStruct((nh, l, V7X_LANES), BF16),
                   jax.ShapeDtypeStruct((nh, nt, ATTN_VROWS, tm), BF16)],
        scratch_shapes=[pltpu.VMEM((1, V7X_LANES), F32)],
        compiler_params=_cparams(("arbitrary",)),
        name="qkv_prompt",
    )(h, g_kv, sh_kv, sc_kv, g_q, sh_q, sc_q, w_kvf, b_f, w_q, k_norm, q_norm, e, et)


def _qkv(h, g_kv, sh_kv, sc_kv, g_q, sh_q, sc_q, w_kvf, b_f, w_q, k_norm, q_norm, e, et, tm,
         q_scale):
    m, d = h.shape
    nkvf = w_kvf.shape[1]
    return pl.pallas_call(
        functools.partial(_qkv_kernel, q_scale=q_scale),
        grid=(m // tm,),
        in_specs=[_row_spec(tm, d),
                  _const_spec((1, d)), _mod_spec(sh_kv, tm), _mod_spec(sc_kv, tm),
                  _const_spec((1, d)), _mod_spec(sh_q, tm), _mod_spec(sc_q, tm),
                  _const_spec((d, nkvf)), _const_spec((1, V7X_LANES)), _const_spec((d, d)),
                  _const_spec((1, d)), _const_spec((1, d)),
                  _const_spec((d, V7X_LANES)), _const_spec((V7X_LANES, d))],
        out_specs=[_row_spec(tm, d), _row_spec(tm, d), _row_spec(tm, N_HEADS), _row_spec(tm, d)],
        out_shape=[jax.ShapeDtypeStruct((m, d), F32), jax.ShapeDtypeStruct((m, d), F32),
                   jax.ShapeDtypeStruct((m, N_HEADS), F32), jax.ShapeDtypeStruct((m, d), BF16)],
        compiler_params=_cparams(("arbitrary",)),
        name="qkv",
    )(h, g_kv, sh_kv, sc_kv, g_q, sh_q, sc_q, w_kvf, b_f, w_q, k_norm, q_norm, e, et)


LOG2E = 1.4426950408889634
ATTN_AUG = 3
ATTN_VROWS = 80
ATTN_QCOLS = 256


def _attn_kernel(js_ref, q_ref, k_ref, v_ref, o_ref, sa_scr, sb_scr, m_scr, acc_scr, *, hd):
    i = pl.program_id(1)
    js = js_ref[pl.program_id(0) * pl.num_programs(1) + i]
    n_full = i - js
    q_t = q_ref[...]
    m_scr[...] = jnp.full(m_scr.shape, -jnp.inf, F32)
    acc_scr[...] = jnp.zeros(acc_scr.shape, F32)

    def scores(j, dst):
        dst[...] = jnp.dot(k_ref[j], q_t, preferred_element_type=F32)

    def consume(j, src, masked):
        tq = src.shape[1]
        for c0 in range(0, tq, ATTN_QCOLS):
            cols = slice(c0, c0 + ATTN_QCOLS)
            s_t = src[:, cols]
            if masked:
                key = lax.broadcasted_iota(jnp.int32, s_t.shape, 0)
                qry = lax.broadcasted_iota(jnp.int32, s_t.shape, 1) + c0
                s_t = jnp.where(key <= qry, s_t, -jnp.inf)
            m_old = m_scr[:, cols]
            m_new = jnp.maximum(m_old, jnp.max(s_t, axis=0, keepdims=True))
            alpha = jnp.exp2(m_old - m_new)
            p_t = jnp.exp2(s_t - m_new).astype(BF16)
            acc_scr[:, cols] = alpha * acc_scr[:, cols] + jnp.dot(v_ref[j], p_t,
                                                                  preferred_element_type=F32)
            m_scr[:, cols] = m_new

    scores(js, sa_scr)

    def pair(jj, c):
        t0 = js + 2 * jj
        scores(t0 + 1, sb_scr)
        consume(t0, sa_scr, False)
        scores(t0 + 2, sa_scr)
        consume(t0 + 1, sb_scr, False)
        return c

    lax.fori_loop(0, lax.shift_right_logical(n_full, 1), pair, 0)

    @pl.when((n_full & 1) == 0)
    def _():
        consume(i, sa_scr, True)

    @pl.when((n_full & 1) == 1)
    def _():
        scores(i, sb_scr)
        consume(i - 1, sa_scr, False)
        consume(i, sb_scr, True)

    acc = acc_scr[...]
    o_ref[...] = (acc[:hd] / acc[hd:hd + 1]).astype(o_ref.dtype)


ATTN_DEAD_LOG2 = 152.0


def _first_live_tile(q_t, kh, f_log2, tq):
    nh, _, l = q_t.shape
    nt = l // tq
    qn = jnp.sqrt(jnp.sum(jnp.square(q_t.astype(F32)), axis=1))
    kn = jnp.sqrt(jnp.sum(jnp.square(kh.astype(F32)), axis=2))
    qn_t = jnp.max(qn.reshape(nh, nt, tq), axis=2)
    kn_t = jnp.max(kn.reshape(nh, nt, tq), axis=2)
    k_max = jnp.max(kn_t, axis=1, keepdims=True)
    f = f_log2.T.reshape(nh, nt, tq)
    reach = 1.01 * qn_t * (k_max + kn_t) + f[:, :, 0]
    gap = reach[:, :, None] - f[:, None, :, tq - 1]
    below = jnp.arange(nt)[None, :] < jnp.arange(nt)[:, None]
    dead = ((gap < -ATTN_DEAD_LOG2) & below[None]).astype(jnp.int32)
    return jnp.sum(jnp.cumprod(dead, axis=2), axis=2).reshape(-1)


def _attn_prompt(q_aug, k_aug, v_aug, fcum, hd):
    nh, nt, _, tq = v_aug.shape
    l = nt * tq
    d = nh * hd
    js = _first_live_tile(q_aug[:, :hd, :], k_aug[:, :, :hd], fcum * LOG2E, tq)
    k_aug = k_aug.reshape(nh, nt, tq, V7X_LANES)
    o_t = pl.pallas_call(
        functools.partial(_attn_kernel, hd=hd),
        grid_spec=pltpu.PrefetchScalarGridSpec(
            num_scalar_prefetch=1,
            grid=(nh, nt),
            in_specs=[pl.BlockSpec((None, V7X_LANES, tq), lambda h, i, js_ref: (h, 0, i)),
                      pl.BlockSpec((None, nt, tq, V7X_LANES), lambda h, i, js_ref: (h, 0, 0, 0)),
                      pl.BlockSpec((None, nt, ATTN_VROWS, tq), lambda h, i, js_ref: (h, 0, 0, 0))],
            out_specs=pl.BlockSpec((None, hd, tq), lambda h, i, js_ref: (h, 0, i)),
            scratch_shapes=[pltpu.VMEM((tq, tq), F32), pltpu.VMEM((tq, tq), F32),
                            pltpu.VMEM((1, tq), F32), pltpu.VMEM((ATTN_VROWS, tq), F32)]),
        out_shape=jax.ShapeDtypeStruct((nh, hd, l), BF16),
        compiler_params=_cparams(("arbitrary", "arbitrary")),
        name="attn_prompt",
    )(js, q_aug, k_aug, v_aug)
    return o_t.transpose(2, 0, 1).reshape(l, d)


def _attn_paged_kernel(pt_ref, qb_ref, knew_ref, vnew_ref, lfnew_ref, *refs, pages_per_step, hd):
    del pt_ref
    pps = pages_per_step
    k_refs = refs[:pps]
    v_refs = refs[pps:2 * pps]
    lf_refs = refs[2 * pps:3 * pps]
    o_ref, m_scr, l_scr, f_scr, s_scr, p_scr, a_scr, acc_scr = refs[3 * pps:]
    nh = N_HEADS
    d, ps = acc_scr.shape
    step = pl.program_id(1)

    @pl.when(step == 0)
    def _():
        m_scr[...] = jnp.full(m_scr.shape, -jnp.inf, F32)
        l_scr[...] = jnp.zeros(l_scr.shape, F32)
        f_scr[...] = jnp.zeros(f_scr.shape, F32)
        acc_scr[...] = jnp.zeros(acc_scr.shape, F32)

    def update(k_rows, v_rows, fs, n_valid):
        n = len(fs)
        for r in range(n):
            for h in range(nh):
                rows = slice(h * hd, (h + 1) * hd)
                s_scr[h:h + 1, r * ps:(r + 1) * ps] = jnp.sum(k_rows(r, rows) * qb_ref[rows, :],
                                                             axis=0, keepdims=True)
        z = s_scr[:, :n * ps] - jnp.concatenate(fs, axis=1)
        if n_valid is not None:
            z = jnp.where(lax.broadcasted_iota(jnp.int32, z.shape, 1) % ps < n_valid, z, -jnp.inf)
        m_old = m_scr[...]
        m_new = jnp.maximum(m_old, jnp.max(z, axis=1, keepdims=True))
        alpha = jnp.exp(m_old - m_new)
        p = jnp.exp(z - m_new)
        l_scr[...] = alpha * l_scr[...] + jnp.sum(p, axis=1, keepdims=True)
        m_scr[...] = m_new
        p_scr[:, :n * ps] = p
        a_scr[...] = jnp.broadcast_to(alpha, (nh, ps))
        for h in range(nh):
            rows = slice(h * hd, (h + 1) * hd)
            acc = jnp.broadcast_to(a_scr[h:h + 1, :], (hd, ps)) * acc_scr[rows, :]
            for r in range(n):
                acc = acc + jnp.broadcast_to(p_scr[h:h + 1, r * ps:(r + 1) * ps], (hd, ps)) * v_rows(r, rows)
            acc_scr[rows, :] = acc

    tri = (lax.broadcasted_iota(jnp.int32, (ps, ps), 0)
           <= lax.broadcasted_iota(jnp.int32, (ps, ps), 1)).astype(F32)
    fs = []
    carry = f_scr[...]
    for r in range(pps):
        fs.append(jnp.dot(lf_refs[r][...], tri, precision=HIGHEST, preferred_element_type=F32) + carry)
        carry = fs[-1][:, ps - 1:ps]
    update(lambda r, rows: k_refs[r][rows, :], lambda r, rows: v_refs[r][rows, :], fs, None)
    f_scr[...] = carry

    @pl.when(step == pl.num_programs(1) - 1)
    def _():
        f_new = jnp.broadcast_to(f_scr[...] + lfnew_ref[...], (nh, ps))
        update(lambda r, rows: jnp.broadcast_to(knew_ref[rows, :], (hd, ps)),
               lambda r, rows: jnp.broadcast_to(vnew_ref[rows, :], (hd, ps)), [f_new], 1)
        for h in range(nh):
            rows = slice(h * hd, (h + 1) * hd)
            o_h = jnp.sum(acc_scr[rows, :], axis=1, keepdims=True)
            o_ref[rows, :] = o_h / l_scr[h:h + 1, :]


def _attn_paged(q_bf16, k_new, v_new, lf_new, cache_k, cache_v, cache_logf, page_table,
                pages_per_step=16):
    b, d = k_new.shape
    nh = N_HEADS
    hd = d // nh
    n_pages = page_table.shape[1]
    n_phys, ps = cache_k.shape[0], cache_k.shape[1]
    pps = math.gcd(pages_per_step, n_pages)
    ck = cache_k.transpose(0, 2, 3, 1).reshape(n_phys, d, ps)
    cv = cache_v.transpose(0, 2, 3, 1).reshape(n_phys, d, ps)
    clf = cache_logf.transpose(0, 2, 1).astype(F32)
    qb = jnp.broadcast_to(q_bf16.astype(F32)[:, :, None], (b, d, ps))
    pt = page_table.reshape(-1).astype(jnp.int32)

    def page_map(r):
        return lambda s, g, pt_ref: (pt_ref[s * n_pages + g * pps + r], 0, 0)

    seq = lambda rows, width: pl.BlockSpec((None, rows, width), lambda s, g, pt_ref: (s, 0, 0))
    in_specs = ([seq(d, ps), seq(d, 1), seq(d, 1), seq(nh, 1)]
                + [pl.BlockSpec((None, d, ps), page_map(r % pps)) for r in range(2 * pps)]
                + [pl.BlockSpec((None, nh, ps), page_map(r)) for r in range(pps)])
    out = pl.pallas_call(
        functools.partial(_attn_paged_kernel, pages_per_step=pps, hd=hd),
        grid_spec=pltpu.PrefetchScalarGridSpec(
            num_scalar_prefetch=1,
            grid=(b, n_pages // pps),
            in_specs=in_specs,
            out_specs=seq(d, 1),
            scratch_shapes=[pltpu.VMEM((nh, 1), F32), pltpu.VMEM((nh, 1), F32), pltpu.VMEM((nh, 1), F32),
                            pltpu.VMEM((nh, pps * ps), F32), pltpu.VMEM((nh, pps * ps), F32),
                            pltpu.VMEM((nh, ps), F32), pltpu.VMEM((d, ps), F32)]),
        out_shape=jax.ShapeDtypeStruct((b, d, 1), F32),
        compiler_params=_cparams(("arbitrary", "arbitrary")),
        name="attn_paged",
    )(pt, qb, k_new.reshape(b, d, 1), v_new.reshape(b, d, 1), lf_new.reshape(b, nh, 1),
      *([ck] * pps), *([cv] * pps), *([clf] * pps))
    return out.reshape(b, d).astype(BF16)


def _row_tile(m):
    return min(m, 512)


def _layer_mods(mod_all, kv_all, rows, d):
    per_layer = [[mod_all[layer, rows, i * d:(i + 1) * d] for i in range(6)]
                 for layer in range(mod_all.shape[0])]
    kv = [kv_all[0, rows, i * d:(i + 1) * d] for i in range(2)]
    return per_layer, kv


def kernel(x_prompt, x_sample, c_prompt, c_sample, state_ssm_re, state_ssm_im, cache_k, cache_v, cache_logf, page_table, ada_w, ada_b, g_mix, g_mlp, mlp_w1, mlp_w2, ssm_lam_re, ssm_lam_im, ssm_log_dt, ssm_b_re, ssm_b_im, ssm_c_re, ssm_c_im, ssm_d, w_glu, kv_ada_w, kv_ada_b, g_kv, w_kvf, b_f, k_norm, w_q, q_norm, w_o):
    bp, l, d = x_prompt.shape
    bs = x_sample.shape[0]
    nh = N_HEADS
    hd = d // nh
    assert bp == 1 and x_sample.shape[1] == 1 and ada_w.shape[0] == 2
    assert l % (S5_SEGMENTS * S5_ROWS_PER_SEG) == 0 and S5_ROWS_PER_SEG % (2 * S5_CHUNK) == 0

    row = lambda a: a.reshape(1, -1).astype(F32)
    w_glu_b = w_glu[0].astype(BF16)
    w1_b = mlp_w1.astype(BF16)
    w2_b = mlp_w2.astype(BF16)
    w_q_b = w_q[0].astype(BF16)
    w_o_b = w_o[0].astype(BF16)
    w_kvf_b = jnp.concatenate(
        [w_kvf, jnp.zeros((d, V7X_LANES - nh), w_kvf.dtype)], axis=1).astype(BF16)
    b_f_p = jnp.concatenate([b_f.astype(F32), jnp.zeros((V7X_LANES - nh,), F32)]).reshape(1, -1)
    head_of = jnp.arange(d) // hd
    e = (head_of[:, None] == jnp.arange(V7X_LANES)[None, :]).astype(F32)
    e_mean = (e / hd).astype(BF16)
    e_t = e.T.astype(BF16)
    k_norm_full = jnp.tile(k_norm.astype(F32), nh).reshape(1, d)
    q_norm_full = jnp.tile(q_norm[0].astype(F32), nh).reshape(1, d)

    ab_re, ab_im, bb_re, bb_im = _s5_discretise(ssm_lam_re[0], ssm_lam_im[0], ssm_log_dt[0],
                                                ssm_b_re[0], ssm_b_im[0])
    n_local = l // S5_CHUNK // S5_SEGMENTS
    s5_w = _s5_chunk_weights(ab_re, ab_im, bb_re, bb_im, ssm_c_re[0], ssm_c_im[0], n_local)

    n_c = bp + bs
    c_rows = -(-n_c // 8) * 8
    c_all = jnp.concatenate([c_prompt, c_sample, jnp.zeros((c_rows - n_c, d), F32)], axis=0)
    mod_all = _mods(c_all, ada_w, ada_b.reshape(ada_b.shape[0], 1, -1))
    kv_all = _mods(c_all, kv_ada_w[None], kv_ada_b.reshape(1, 1, -1))

    def trunk(x, rows, s5_fn, qkv_attend):
        m = x.shape[0]
        tm = _row_tile(m)
        (l0, l1), (sh_kv, sc_kv) = _layer_mods(mod_all, kv_all, rows, d)
        sh_m, sc_m, g_m, sh_f, sc_f, g_f = l0
        mlp0 = (row(g_mlp[0]), sh_f, sc_f, g_f, w1_b[0], w2_b[0])
        h, st_re, st_im = s5_fn(x, row(g_mix[0]), sh_m, sc_m, g_m, mlp0, tm)
        sh_m, sc_m, g_m, sh_f, sc_f, g_f = l1
        qkv_args = (h, row(g_kv), sh_kv, sc_kv, row(g_mix[1]), sh_m, sc_m,
                    w_kvf_b, b_f_p, w_q_b, k_norm_full, q_norm_full, e_mean, e_t, tm)
        k, v, lf, o = qkv_attend(qkv_args)
        h = _oproj_mlp(o, h, g_m, w_o_b, row(g_mlp[1]), sh_f, sc_f, g_f, w1_b[1], w2_b[1], tm)
        return h, st_re, st_im, k, v, lf

    def s5_prompt(x, g, sh, sc, gate, mlp0, tm):
        y, st_re, st_im = _s5_prompt(_ada_rms_chunked(x, g, sh, sc), s5_w)
        return _glu_mlp_chunked(x, y, g, row(ssm_d[0]), sh, sc, gate, w_glu_b, *mlp0), st_re, st_im

    def attend_prompt(qkv_args):
        k_t, v_t, lf_t, fcum, q_aug, k_aug, v_aug = _qkv_prompt(*qkv_args, LOG2E / math.sqrt(hd))
        o = _attn_prompt(q_aug, k_aug, v_aug, fcum, hd)
        tokens_last = lambda a: a.reshape(nh, hd, l).transpose(2, 0, 1)[None]
        return tokens_last(k_t), tokens_last(v_t), lf_t.T[None], o

    y_p, sre_p, sim_p, k_p, v_p, f_p = trunk(x_prompt[0], slice(0, 1), s5_prompt, attend_prompt)

    def s5_sample(x, g, sh, sc, gate, mlp0, tm):
        u = _ada_rms(x, g, sh, sc, tm, F32)
        y, st_re, st_im = _s5_step(u, state_ssm_re[0], state_ssm_im[0], ab_re, ab_im, bb_re, bb_im,
                                   ssm_c_re[0], ssm_c_im[0])
        h = _glu(x, y, g, row(ssm_d[0]), sh, sc, gate, w_glu_b, tm)
        return _mlp(h, *mlp0, tm), st_re, st_im

    def attend_sample(qkv_args):
        k, v, lf, q = _qkv(*qkv_args, 1.0 / math.sqrt(hd))
        o = _attn_paged(q, k, v, lf, cache_k, cache_v, cache_logf, page_table)
        return k.reshape(bs, 1, nh, hd), v.reshape(bs, 1, nh, hd), lf[:, None], o

    y_s, sre_s, sim_s, k_s, v_s, f_s = trunk(x_sample[:, 0], slice(bp, bp + bs), s5_sample,
                                             attend_sample)

    sdt = state_ssm_re.dtype
    return (y_p[None], y_s[:, None],
            sre_p[None, None].astype(sdt), sim_p[None, None].astype(sdt), k_p, v_p, f_p,
            sre_s[None].astype(sdt), sim_s[None].astype(sdt), k_s, v_s, f_s)
```

```python
import functools
import math

import jax
import jax.numpy as jnp
from jax import lax
from jax.experimental import pallas as pl
from jax.experimental.pallas import tpu as pltpu

F32 = jnp.float32
BF16 = jnp.bfloat16
HIGHEST = lax.Precision.HIGHEST

EPS = 1e-6
GROUP_SIZE = 16
STATE_DIM = 64
N_HEADS = 16
PAGE_SIZE = 128
S5_CHUNK = 8
S5_SEGMENTS = 8
S5_ROWS_PER_SEG = 64
V7X_LANES = 128
V7X_VMEM_LIMIT = 56 * 1024 * 1024


def _cparams(sem):
    return pltpu.CompilerParams(dimension_semantics=sem, vmem_limit_bytes=V7X_VMEM_LIMIT)


def _const_spec(shape):
    nd = len(shape)
    return pl.BlockSpec(shape, lambda *_: (0,) * nd, pipeline_mode=pl.Buffered(1))


def _row_spec(tm, width):
    return pl.BlockSpec((tm, width), lambda i: (i, 0))


def _mod_spec(arr, tm):
    if arr.shape[0] == 1:
        return pl.BlockSpec((1, arr.shape[1]), lambda i: (0, 0))
    return pl.BlockSpec((tm, arr.shape[1]), lambda i: (i, 0))


def _rms_hat(x):
    return x * lax.rsqrt(jnp.mean(x * x, axis=-1, keepdims=True) + EPS)


def _mods_kernel(c_ref, w_ref, b_ref, o_ref):
    o_ref[...] = jnp.dot(c_ref[...], w_ref[...], precision=HIGHEST,
                         preferred_element_type=F32) + b_ref[...]


def _mods(c, w3, b3, tn=2048):
    nl, d, n = w3.shape
    mp = c.shape[0]
    return pl.pallas_call(
        _mods_kernel,
        grid=(nl, n // tn),
        in_specs=[pl.BlockSpec((mp, d), lambda l, j: (0, 0)),
                  pl.BlockSpec((None, d, tn), lambda l, j: (l, 0, j)),
                  pl.BlockSpec((None, 1, tn), lambda l, j: (l, 0, j))],
        out_specs=pl.BlockSpec((None, mp, tn), lambda l, j: (l, 0, j)),
        out_shape=jax.ShapeDtypeStruct((nl, mp, n), F32),
        compiler_params=_cparams(("arbitrary", "arbitrary")),
        name="mods",
    )(c, w3, b3)


def _ada_rms_kernel(x_ref, g_ref, sh_ref, sc_ref, o_ref):
    u = (_rms_hat(x_ref[...]) * g_ref[...]) * (1.0 + sc_ref[...]) + sh_ref[...]
    o_ref[...] = u.astype(o_ref.dtype)


def _ada_rms(x, g, sh, sc, tm, out_dtype):
    m, d = x.shape
    return pl.pallas_call(
        _ada_rms_kernel,
        grid=(m // tm,),
        in_specs=[_row_spec(tm, d), _const_spec((1, d)), _mod_spec(sh, tm), _mod_spec(sc, tm)],
        out_specs=_row_spec(tm, d),
        out_shape=jax.ShapeDtypeStruct((m, d), out_dtype),
        compiler_params=_cparams(("arbitrary",)),
        name="ada_rms",
    )(x, g, sh, sc)


def _chunk_rows(j, c, rows_per_seg):
    return pl.ds(c * S5_CHUNK + j, S5_SEGMENTS, stride=rows_per_seg)


def _ada_rms_chunked_kernel(x_ref, g_ref, sh_ref, sc_ref, o_ref, u_scr):
    seg, rps, d = x_ref.shape
    x = x_ref[...].reshape(seg * rps, d)
    u = (_rms_hat(x) * g_ref[...]) * (1.0 + sc_ref[...]) + sh_ref[...]
    for k in range(d // V7X_LANES):
        u_scr[k] = u[:, k * V7X_LANES:(k + 1) * V7X_LANES]
        for j in range(S5_CHUNK):
            cols = slice(j * V7X_LANES, (j + 1) * V7X_LANES)
            for c in range(0, rps // S5_CHUNK, 2):
                pair = jnp.concatenate([u_scr[k, _chunk_rows(j, c, rps), :],
                                        u_scr[k, _chunk_rows(j, c + 1, rps), :]], axis=0)
                o_ref[k, c * seg:(c + 2) * seg, cols] = pair.astype(o_ref.dtype)


def _ada_rms_chunked(x, g, sh, sc):
    l, d = x.shape
    seg, rps, t = S5_SEGMENTS, S5_ROWS_PER_SEG, S5_CHUNK
    nb = d // V7X_LANES
    return pl.pallas_call(
        _ada_rms_chunked_kernel,
        grid=(l // (seg * rps),),
        in_specs=[pl.BlockSpec((seg, rps, d), lambda i: (0, i, 0)), _const_spec((1, d)),
                  _const_spec((1, d)), _const_spec((1, d))],
        out_specs=pl.BlockSpec((nb, seg * rps // t, t * V7X_LANES), lambda i: (0, i, 0)),
        out_shape=jax.ShapeDtypeStruct((nb, l // t, t * V7X_LANES), BF16),
        scratch_shapes=[pltpu.VMEM((nb, seg * rps, V7X_LANES), F32)],
        compiler_params=_cparams(("arbitrary",)),
        name="ada_rms_chunked",
    )(x.reshape(seg, l // seg, d), g, sh, sc)


def _s5_discretise(lam_re, lam_im, log_dt, b_re, b_im):
    dt = jnp.exp(log_dt.astype(F32))[:, None]
    lr = lam_re.astype(F32)
    li = lam_im.astype(F32)
    mag = jnp.exp(lr * dt)
    ang = li * dt
    ab_re = mag * jnp.cos(ang)
    ab_im = mag * jnp.sin(ang)
    den = lr * lr + li * li
    zr = ((ab_re - 1) * lr + ab_im * li) / den
    zi = (ab_im * lr - (ab_re - 1) * li) / den
    br = b_re.astype(F32)
    bi = b_im.astype(F32)
    bb_re = zr[..., None] * br - zi[..., None] * bi
    bb_im = zr[..., None] * bi + zi[..., None] * br
    return ab_re, ab_im, bb_re, bb_im


def _cpow_table(ar, ai, n):
    tr = jnp.ones((1,) + ar.shape, F32)
    ti = jnp.zeros((1,) + ar.shape, F32)
    br, bi = ar, ai
    while tr.shape[0] < n:
        tr, ti = (jnp.concatenate([tr, tr * br - ti * bi], axis=0),
                  jnp.concatenate([ti, tr * bi + ti * br], axis=0))
        br, bi = br * br - bi * bi, 2.0 * br * bi
    return tr[:n], ti[:n]


def _block_lanes(x_re, x_im, nb):
    def blk(x):
        x = x.reshape((nb, x.shape[0] // nb) + x.shape[1:])
        x = jnp.moveaxis(x, 1, -2)
        return x.reshape(x.shape[:-2] + (x.shape[-2] * x.shape[-1],))
    return jnp.concatenate([blk(x_re), blk(x_im)], axis=-1)


_ein = functools.partial(jnp.einsum, precision=HIGHEST)


def _expand_groups(compact, row_w, col_w, gb):
    nb, r, c = compact.shape
    src = jnp.arange(c)
    dst = jnp.arange(c * gb)
    dst_src = (dst // (col_w * gb)) * col_w + dst % col_w
    rep = (src[:, None] == dst_src[None, :]).astype(BF16)
    wide = jnp.einsum('krc,cd->krd', compact.astype(BF16), rep, preferred_element_type=BF16)
    row_group = (jnp.arange(r) // row_w) % gb
    col_group = (dst // col_w) % gb
    return jnp.where(row_group[:, None] == col_group[None, :], wide, jnp.zeros((), BF16))


def _s5_chunk_weights(ab_re, ab_im, bb_re, bb_im, c_re, c_im, n_local):
    t = S5_CHUNK
    g, p = ab_re.shape
    h = GROUP_SIZE
    gb = V7X_LANES // h
    nb = g // gb
    cr = c_re.astype(F32)
    ci = c_im.astype(F32)
    pw_re, pw_im = _cpow_table(ab_re, ab_im, t + 1)
    cb_rr = _ein('gop,gph->gpoh', cr, bb_re) - _ein('gop,gph->gpoh', ci, bb_im)
    cb_ii = _ein('gop,gph->gpoh', cr, bb_im) + _ein('gop,gph->gpoh', ci, bb_re)
    pw_gt = lambda a: jnp.moveaxis(a[:t], 0, 1)[:, :, :, None, None]
    kern = jnp.sum(pw_gt(pw_re) * cb_rr[:, None] - pw_gt(pw_im) * cb_ii[:, None], axis=2)
    tau = jnp.arange(t)[None, :] - jnp.arange(t)[:, None]
    m = jnp.where((tau >= 0)[None, :, :, None, None], kern[:, jnp.clip(tau, 0, t - 1)], 0.0)
    m = m.reshape(nb, gb, t, t, h, h)
    m_all = _expand_groups(m.transpose(0, 2, 1, 5, 3, 4).reshape(nb, t * gb * h, t * h), h, h, gb)
    rev = t - 1 - jnp.arange(t)
    p_re = _ein('jgp,gph->gjhp', pw_re[rev], bb_re) - _ein('jgp,gph->gjhp', pw_im[rev], bb_im)
    p_im = _ein('jgp,gph->gjhp', pw_re[rev], bb_im) + _ein('jgp,gph->gjhp', pw_im[rev], bb_re)
    p_ri = jnp.stack([p_re, p_im], axis=1).reshape(nb, gb, 2, t, h, p)
    p_all = _expand_groups(p_ri.transpose(0, 3, 1, 4, 2, 5).reshape(nb, t * gb * h, 2 * p), h, p, gb)
    w_re = _ein('gop,tgp->gpto', cr, pw_re[1:]) - _ein('gop,tgp->gpto', ci, pw_im[1:])
    w_im = _ein('gop,tgp->gpto', cr, pw_im[1:]) + _ein('gop,tgp->gpto', ci, pw_re[1:])
    w_ri = jnp.stack([w_re, -w_im], axis=1).reshape(nb, gb, 2, p, t, h)
    q_all = _expand_groups(w_ri.transpose(0, 2, 1, 3, 4, 5).reshape(nb, 2 * gb * p, t * h), p, h, gb)
    at_re, at_im = pw_re[t], pw_im[t]
    lp_re, lp_im = _cpow_table(at_re, at_im, n_local + 1)
    apow = _block_lanes(jnp.moveaxis(lp_re[:n_local], 0, 1), jnp.moveaxis(lp_im[:n_local], 0, 1), nb)
    a_chunk = _block_lanes(at_re[:, None], at_im[:, None], nb)
    a_seg = _block_lanes(lp_re[n_local][:, None], lp_im[n_local][:, None], nb)
    return m_all.astype(BF16), p_all.astype(BF16), q_all.astype(BF16), a_chunk, apow, a_seg


def _s5_prompt_kernel(u_ref, p_ref, m_ref, q_ref, ac_ref, apow_ref, aseg_ref,
                      y_ref, fin_ref, sp_scr, *, n_local):
    seg = S5_SEGMENTS
    half = sp_scr.shape[1] // 2
    sp_scr[...] = jnp.dot(u_ref[...], p_ref[...], preferred_element_type=F32)
    ar = jnp.broadcast_to(ac_ref[:, :half], (seg, half))
    ai = jnp.broadcast_to(ac_ref[:, half:], (seg, half))

    def scan_step(k, carry):
        sr, si = carry
        rows = pl.ds(pl.multiple_of(k * seg, seg), seg)
        gr = sp_scr[rows, :half]
        gi = sp_scr[rows, half:]
        sp_scr[rows, :half] = sr
        sp_scr[rows, half:] = si
        return (ar * sr - ai * si + gr, ar * si + ai * sr + gi)

    zero = jnp.zeros((seg, half), F32)
    er, ei = lax.fori_loop(0, n_local, scan_step, (zero, zero))

    sgr = aseg_ref[:, :half]
    sgi = aseg_ref[:, half:]
    cr = jnp.zeros((1, half), F32)
    ci = jnp.zeros((1, half), F32)
    rows_r, rows_i = [], []
    for s in range(seg):
        rows_r.append(cr)
        rows_i.append(ci)
        cr, ci = (sgr * cr - sgi * ci + er[s:s + 1], sgr * ci + sgi * cr + ei[s:s + 1])
    fin_ref[...] = jnp.concatenate([cr, ci], axis=-1)
    sin_r = jnp.concatenate(rows_r, axis=0)
    sin_i = jnp.concatenate(rows_i, axis=0)

    def fix_step(k, c):
        rows = pl.ds(pl.multiple_of(k * seg, seg), seg)
        apk = apow_ref[pl.ds(k, 1), :]
        pr = jnp.broadcast_to(apk[:, :half], (seg, half))
        pi = jnp.broadcast_to(apk[:, half:], (seg, half))
        sp_scr[rows, :half] += pr * sin_r - pi * sin_i
        sp_scr[rows, half:] += pr * sin_i + pi * sin_r
        return c

    lax.fori_loop(0, n_local, fix_step, 0)
    y = (jnp.dot(sp_scr[...].astype(BF16), q_ref[...], preferred_element_type=F32)
         + jnp.dot(u_ref[...], m_ref[...], preferred_element_type=F32))
    y_ref[...] = y.astype(y_ref.dtype)


def _s5_prompt(u_chunked, weights):
    m_all, p_all, q_all, a_chunk, apow, a_seg = weights
    nb, nc, kw = u_chunked.shape
    n_local = nc // S5_SEGMENTS
    sw = p_all.shape[2]
    blk = lambda rows, cols: pl.BlockSpec((None, rows, cols), lambda k: (k, 0, 0))
    yt, fin = pl.pallas_call(
        functools.partial(_s5_prompt_kernel, n_local=n_local),
        grid=(nb,),
        in_specs=[blk(nc, kw), blk(kw, sw), blk(kw, kw), blk(sw, kw), blk(1, sw), blk(n_local, sw),
                  blk(1, sw)],
        out_specs=[blk(nc, kw), blk(1, sw)],
        out_shape=[jax.ShapeDtypeStruct((nb, nc, kw), BF16), jax.ShapeDtypeStruct((nb, 1, sw), F32)],
        scratch_shapes=[pltpu.VMEM((nc, sw), F32)],
        compiler_params=_cparams(("arbitrary",)),
        name="s5_prompt",
    )(u_chunked, p_all, m_all, q_all, a_chunk, apow, a_seg)
    fin = fin.reshape(nb, 2, -1, STATE_DIM)
    return yt, fin[:, 0].reshape(-1, STATE_DIM), fin[:, 1].reshape(-1, STATE_DIM)


def _s5_step_kernel(u_ref, h_ref, hs_ref, a1_ref, a2_ref, bbt_ref, cq_ref, y_ref, hn_ref):
    bu = jnp.einsum('gbh,ghp->gbp', u_ref[...], bbt_ref[...], precision=HIGHEST,
                    preferred_element_type=F32)
    hn = a1_ref[...] * h_ref[...] + a2_ref[...] * hs_ref[...] + bu
    hn_ref[...] = hn
    y_ref[...] = jnp.einsum('gbp,gph->gbh', hn, cq_ref[...], precision=HIGHEST,
                            preferred_element_type=F32)


def _s5_step(u, h0_re, h0_im, ab_re, ab_im, bb_re, bb_im, c_re, c_im):
    b, d = u.shape
    g, p = ab_re.shape
    h = GROUP_SIZE
    u3 = u.reshape(b, g, h).transpose(1, 0, 2)
    hc = jnp.concatenate([h0_re, h0_im], axis=-1).transpose(1, 0, 2).astype(F32)
    hs = jnp.concatenate([h0_im, h0_re], axis=-1).transpose(1, 0, 2).astype(F32)
    a1 = jnp.concatenate([ab_re, ab_re], axis=-1)[:, None, :]
    a2 = jnp.concatenate([-ab_im, ab_im], axis=-1)[:, None, :]
    bbt = jnp.concatenate([bb_re, bb_im], axis=1).transpose(0, 2, 1)
    cq = jnp.concatenate([c_re.astype(F32), -c_im.astype(F32)], axis=-1).transpose(0, 2, 1)
    y3, hn = pl.pallas_call(
        _s5_step_kernel,
        out_shape=[jax.ShapeDtypeStruct((g, b, h), F32), jax.ShapeDtypeStruct((g, b, 2 * p), F32)],
        compiler_params=pltpu.CompilerParams(vmem_limit_bytes=V7X_VMEM_LIMIT),
        name="s5_step",
    )(u3, hc, hs, a1, a2, bbt, cq)
    y = y3.transpose(1, 0, 2).reshape(b, d)
    hn = hn.transpose(1, 0, 2)
    return y, hn[..., :p], hn[..., p:]


def _glu_body(x, y_ssm, g_ref, d_ref, sh_ref, sc_ref, gate_ref, w_ref):
    dm = x.shape[1]
    u = (_rms_hat(x) * g_ref[...]) * (1.0 + sc_ref[...]) + sh_ref[...]
    y = y_ssm + d_ref[...] * u
    z = jnp.dot(jax.nn.gelu(y).astype(BF16), w_ref[...], preferred_element_type=F32)
    out = z[:, :dm] * jax.nn.sigmoid(z[:, dm:])
    return x + gate_ref[...] * out


def _glu_kernel(x_ref, y_ref, g_ref, d_ref, sh_ref, sc_ref, gate_ref, w_ref, o_ref):
    o_ref[...] = _glu_body(x_ref[...], y_ref[...].astype(F32), g_ref, d_ref, sh_ref, sc_ref,
                           gate_ref, w_ref)


def _glu(x, y_ssm, g, d_skip, sh, sc, gate, w_glu, tm):
    m, d = x.shape
    return pl.pallas_call(
        _glu_kernel,
        grid=(m // tm,),
        in_specs=[_row_spec(tm, d), _row_spec(tm, d), _const_spec((1, d)), _const_spec((1, d)),
                  _mod_spec(sh, tm), _mod_spec(sc, tm), _mod_spec(gate, tm), _const_spec((d, 2 * d))],
        out_specs=_row_spec(tm, d),
        out_shape=jax.ShapeDtypeStruct((m, d), F32),
        compiler_params=_cparams(("arbitrary",)),
        name="glu",
    )(x, y_ssm, g, d_skip, sh, sc, gate, w_glu)


def _glu_chunked_kernel(x_ref, y_ref, g_ref, d_ref, sh_ref, sc_ref, gate_ref, w_ref,
                        g2_ref, sh2_ref, sc2_ref, gate2_ref, w1_ref, w2_ref, o_ref, y_scr):
    seg, rps, d = x_ref.shape
    nb = d // V7X_LANES
    for k in range(nb):
        for j in range(S5_CHUNK):
            cols = slice(j * V7X_LANES, (j + 1) * V7X_LANES)
            for c in range(0, rps // S5_CHUNK, 2):
                pair = y_ref[k, c * seg:(c + 2) * seg, cols].astype(F32)
                y_scr[k, _chunk_rows(j, c, rps), :] = pair[:seg]
                y_scr[k, _chunk_rows(j, c + 1, rps), :] = pair[seg:]
    y_ssm = jnp.concatenate([y_scr[k] for k in range(nb)], axis=1)
    out = _glu_body(x_ref[...].reshape(seg * rps, d), y_ssm, g_ref, d_ref, sh_ref, sc_ref,
                    gate_ref, w_ref)
    out = _mlp_body(out, g2_ref, sh2_ref, sc2_ref, gate2_ref, w1_ref, w2_ref)
    o_ref[...] = out.reshape(seg, rps, d)


def _glu_mlp_chunked(x, y_chunked, g, d_skip, sh, sc, gate, w_glu, g2, sh2, sc2, gate2, w1, w2):
    l, d = x.shape
    dff = w1.shape[1]
    seg, rps, t = S5_SEGMENTS, S5_ROWS_PER_SEG, S5_CHUNK
    nb = d // V7X_LANES
    x_spec = pl.BlockSpec((seg, rps, d), lambda i: (0, i, 0))
    out = pl.pallas_call(
        _glu_chunked_kernel,
        grid=(l // (seg * rps),),
        in_specs=[x_spec, pl.BlockSpec((nb, seg * rps // t, t * V7X_LANES), lambda i: (0, i, 0)),
                  _const_spec((1, d)), _const_spec((1, d)), _const_spec((1, d)), _const_spec((1, d)),
                  _const_spec((1, d)), _const_spec((d, 2 * d)),
                  _const_spec((1, d)), _const_spec((1, d)), _const_spec((1, d)), _const_spec((1, d)),
                  _const_spec((d, dff)), _const_spec((dff, d))],
        out_specs=x_spec,
        out_shape=jax.ShapeDtypeStruct((seg, l // seg, d), F32),
        scratch_shapes=[pltpu.VMEM((nb, seg * rps, V7X_LANES), F32)],
        compiler_params=_cparams(("arbitrary",)),
        name="glu_mlp_chunked",
    )(x.reshape(seg, l // seg, d), y_chunked, g, d_skip, sh, sc, gate, w_glu, g2, sh2, sc2, gate2, w1, w2)
    return out.reshape(l, d)


MLP_FF_CHUNK = 1024


def _mlp_body(x, g_ref, sh_ref, sc_ref, gate_ref, w1_ref, w2_ref):
    xn = ((_rms_hat(x) * g_ref[...]) * (1.0 + sc_ref[...]) + sh_ref[...]).astype(BF16)
    acc = jnp.zeros(x.shape, F32)
    for c in range(0, w1_ref.shape[1], MLP_FF_CHUNK):
        a = jnp.dot(xn, w1_ref[:, c:c + MLP_FF_CHUNK], preferred_element_type=F32)
        a = jnp.square(jnp.maximum(a, 0.0)).astype(BF16)
        acc = acc + jnp.dot(a, w2_ref[c:c + MLP_FF_CHUNK, :], preferred_element_type=F32)
    return x + gate_ref[...] * acc


def _mlp_kernel(x_ref, g_ref, sh_ref, sc_ref, gate_ref, w1_ref, w2_ref, o_ref):
    o_ref[...] = _mlp_body(x_ref[...], g_ref, sh_ref, sc_ref, gate_ref, w1_ref, w2_ref)


def _oproj_mlp_kernel(o_ref, h_ref, gate_m_ref, wo_ref, g_ref, sh_ref, sc_ref, gate_ref, w1_ref, w2_ref,
                      out_ref):
    x = h_ref[...] + gate_m_ref[...] * jnp.dot(o_ref[...], wo_ref[...], preferred_element_type=F32)
    out_ref[...] = _mlp_body(x, g_ref, sh_ref, sc_ref, gate_ref, w1_ref, w2_ref)


def _oproj_mlp(o, h, gate_m, w_o, g, sh, sc, gate, w1, w2, tm):
    m, d = h.shape
    dff = w1.shape[1]
    return pl.pallas_call(
        _oproj_mlp_kernel,
        grid=(m // tm,),
        in_specs=[_row_spec(tm, d), _row_spec(tm, d), _mod_spec(gate_m, tm), _const_spec((d, d)),
                  _const_spec((1, d)), _mod_spec(sh, tm), _mod_spec(sc, tm), _mod_spec(gate, tm),
                  _const_spec((d, dff)), _const_spec((dff, d))],
        out_specs=_row_spec(tm, d),
        out_shape=jax.ShapeDtypeStruct((m, d), F32),
        compiler_params=_cparams(("arbitrary",)),
        name="oproj_mlp",
    )(o, h, gate_m, w_o, g, sh, sc, gate, w1, w2)


def _mlp(x, g, sh, sc, gate, w1, w2, tm):
    m, d = x.shape
    dff = w1.shape[1]
    return pl.pallas_call(
        _mlp_kernel,
        grid=(m // tm,),
        in_specs=[_row_spec(tm, d), _const_spec((1, d)), _mod_spec(sh, tm), _mod_spec(sc, tm),
                  _mod_spec(gate, tm), _const_spec((d, dff)), _const_spec((dff, d))],
        out_specs=_row_spec(tm, d),
        out_shape=jax.ShapeDtypeStruct((m, d), F32),
        compiler_params=_cparams(("arbitrary",)),
        name="mlp",
    )(x, g, sh, sc, gate, w1, w2)


def _head_norm(t, gain, e_ref, et_ref):
    msq = jnp.dot((t * t).astype(BF16), e_ref[...], preferred_element_type=F32)
    r = lax.rsqrt(msq + EPS)
    r_hi = r.astype(BF16)
    r_lo = (r - r_hi.astype(F32)).astype(BF16)
    r_full = (jnp.dot(r_hi, et_ref[...], preferred_element_type=F32)
              + jnp.dot(r_lo, et_ref[...], preferred_element_type=F32))
    return (t * r_full) * gain


def _qkv_kernel(h_ref, gkv_ref, shkv_ref, sckv_ref, gq_ref, shq_ref, scq_ref,
                wkvf_ref, bf_ref, wq_ref, kn_ref, qn_ref, e_ref, et_ref,
                k_ref, v_ref, lf_ref, q_ref, *, q_scale):
    x = h_ref[...]
    dm = x.shape[1]
    xh = _rms_hat(x)
    xkv = ((xh * gkv_ref[...]) * (1.0 + sckv_ref[...]) + shkv_ref[...]).astype(BF16)
    xq = ((xh * gq_ref[...]) * (1.0 + scq_ref[...]) + shq_ref[...]).astype(BF16)
    proj = jnp.dot(xkv, wkvf_ref[...], preferred_element_type=F32)
    k_ref[...] = _head_norm(proj[:, :dm], kn_ref[...], e_ref, et_ref)
    v_ref[...] = proj[:, dm:2 * dm]
    lf = jax.nn.log_sigmoid(proj[:, 2 * dm:] + bf_ref[...])
    lf_ref[...] = lf[:, :lf_ref.shape[1]]
    q = _head_norm(jnp.dot(xq, wq_ref[...], preferred_element_type=F32), qn_ref[...], e_ref, et_ref)
    q_ref[...] = (q * q_scale).astype(q_ref.dtype)


def _trunc_bf16(x):
    bits = lax.bitcast_convert_type(x, jnp.uint32) & jnp.uint32(0xFFFF0000)
    return lax.bitcast_convert_type(bits, F32)


def _qkv_prompt_kernel(h_ref, gkv_ref, shkv_ref, sckv_ref, gq_ref, shq_ref, scq_ref,
                       wkvf_ref, bf_ref, wq_ref, kn_ref, qn_ref, e_ref, et_ref,
                       kt_ref, vt_ref, lft_ref, f_ref, qa_ref, ka_ref, va_ref, carry_scr, *, q_scale):
    @pl.when(pl.program_id(0) == 0)
    def _():
        carry_scr[...] = jnp.zeros_like(carry_scr)

    x = h_ref[...]
    tm, dm = x.shape
    nh = N_HEADS
    hd = dm // nh
    xh = _rms_hat(x)
    xkv = ((xh * gkv_ref[...]) * (1.0 + sckv_ref[...]) + shkv_ref[...]).astype(BF16)
    xq = ((xh * gq_ref[...]) * (1.0 + scq_ref[...]) + shq_ref[...]).astype(BF16)
    proj = jnp.dot(xkv, wkvf_ref[...], preferred_element_type=F32)
    k = _head_norm(proj[:, :dm], kn_ref[...], e_ref, et_ref)
    v_t = proj[:, dm:2 * dm].T
    lf = jax.nn.log_sigmoid(proj[:, 2 * dm:] + bf_ref[...])
    q = _head_norm(jnp.dot(xq, wq_ref[...], preferred_element_type=F32), qn_ref[...], e_ref, et_ref)
    q_t = (q * q_scale).T
    kt_ref[...] = k.T
    vt_ref[...] = v_t
    lft_ref[...] = lf.T[:nh]
    tri = (lax.broadcasted_iota(jnp.int32, (tm, tm), 1)
           <= lax.broadcasted_iota(jnp.int32, (tm, tm), 0)).astype(F32)
    f = jnp.dot(tri, lf, precision=HIGHEST, preferred_element_type=F32) + carry_scr[...]
    carry_scr[...] = f[tm - 1:tm, :]
    f_ref[...] = f[:, :nh]

    q_pad = (lax.broadcasted_iota(jnp.int32, (V7X_LANES - hd, tm), 0) < ATTN_AUG).astype(BF16)
    v_pad = (lax.broadcasted_iota(jnp.int32, (ATTN_VROWS - hd, tm), 0) < 1).astype(BF16)
    lane = lax.broadcasted_iota(jnp.int32, (tm, V7X_LANES), 1)
    fs = f * (-LOG2E)
    for h in range(nh):
        rows = slice(h * hd, (h + 1) * hd)
        qa_ref[h, :hd, :] = q_t[rows].astype(BF16)
        qa_ref[h, hd:, :] = q_pad
        va_ref[h, :hd, :] = v_t[rows].astype(BF16)
        va_ref[h, hd:, :] = v_pad
        two = (2 * hd) * (h // 2)
        base = k[:, two:two + 2 * hd]
        if h % 2:
            base = pltpu.roll(base, hd, axis=1)
        hi = _trunc_bf16(jnp.broadcast_to(fs[:, h:h + 1], (tm, V7X_LANES)))
        rest = jnp.broadcast_to(fs[:, h:h + 1], (tm, V7X_LANES)) - hi
        mid = _trunc_bf16(rest)
        lo = _trunc_bf16(rest - mid)
        aug = jnp.where(lane == hd, hi, jnp.where(lane == hd + 1, mid, jnp.where(lane == hd + 2, lo, 0.0)))
        ka_ref[h] = jnp.where(lane < hd, base, aug).astype(BF16)


def _qkv_prompt(h, g_kv, sh_kv, sc_kv, g_q, sh_q, sc_q, w_kvf, b_f, w_q, k_norm, q_norm, e, et, tm,
                q_scale):
    l, d = h.shape
    nh = N_HEADS
    nkvf = w_kvf.shape[1]
    nt = l // tm
    col = lambda rows: pl.BlockSpec((rows, tm), lambda i: (0, i))
    return pl.pallas_call(
        functools.partial(_qkv_prompt_kernel, q_scale=q_scale),
        grid=(nt,),
        in_specs=[_row_spec(tm, d),
                  _const_spec((1, d)), _const_spec((1, d)), _const_spec((1, d)),
                  _const_spec((1, d)), _const_spec((1, d)), _const_spec((1, d)),
                  _const_spec((d, nkvf)), _const_spec((1, V7X_LANES)), _const_spec((d, d)),
                  _const_spec((1, d)), _const_spec((1, d)),
                  _const_spec((d, V7X_LANES)), _const_spec((V7X_LANES, d))],
        out_specs=[col(d), col(d), col(nh), _row_spec(tm, nh),
                   pl.BlockSpec((nh, V7X_LANES, tm), lambda i: (0, 0, i)),
                   pl.BlockSpec((nh, tm, V7X_LANES), lambda i: (0, i, 0)),
                   pl.BlockSpec((nh, None, ATTN_VROWS, tm), lambda i: (0, i, 0, 0))],
        out_shape=[jax.ShapeDtypeStruct((d, l), F32), jax.ShapeDtypeStruct((d, l), F32),
                   jax.ShapeDtypeStruct((nh, l), F32), jax.ShapeDtypeStruct((l, nh), F32),
                   jax.ShapeDtypeStruct((nh, V7X_LANES, l), BF16),
                   jax.ShapeDtypeStruct((nh, l, V7X_LANES), BF16),
                   jax.ShapeDtypeStruct((nh, nt, ATTN_VROWS, tm), BF16)],
        scratch_shapes=[pltpu.VMEM((1, V7X_LANES), F32)],
        compiler_params=_cparams(("arbitrary",)),
        name="qkv_prompt",
    )(h, g_kv, sh_kv, sc_kv, g_q, sh_q, sc_q, w_kvf, b_f, w_q, k_norm, q_norm, e, et)


def _qkv(h, g_kv, sh_kv, sc_kv, g_q, sh_q, sc_q, w_kvf, b_f, w_q, k_norm, q_norm, e, et, tm,
         q_scale):
    m, d = h.shape
    nkvf = w_kvf.shape[1]
    return pl.pallas_call(
        functools.partial(_qkv_kernel, q_scale=q_scale),
        grid=(m // tm,),
        in_specs=[_row_spec(tm, d),
                  _const_spec((1, d)), _mod_spec(sh_kv, tm), _mod_spec(sc_kv, tm),
                  _const_spec((1, d)), _mod_spec(sh_q, tm), _mod_spec(sc_q, tm),
                  _const_spec((d, nkvf)), _const_spec((1, V7X_LANES)), _const_spec((d, d)),
                  _const_spec((1, d)), _const_spec((1, d)),
                  _const_spec((d, V7X_LANES)), _const_spec((V7X_LANES, d))],
        out_specs=[_row_spec(tm, d), _row_spec(tm, d), _row_spec(tm, N_HEADS), _row_spec(tm, d)],
        out_shape=[jax.ShapeDtypeStruct((m, d), F32), jax.ShapeDtypeStruct((m, d), F32),
                   jax.ShapeDtypeStruct((m, N_HEADS), F32), jax.ShapeDtypeStruct((m, d), BF16)],
        compiler_params=_cparams(("arbitrary",)),
        name="qkv",
    )(h, g_kv, sh_kv, sc_kv, g_q, sh_q, sc_q, w_kvf, b_f, w_q, k_norm, q_norm, e, et)


LOG2E = 1.4426950408889634
ATTN_AUG = 3
ATTN_VROWS = 80
ATTN_QCOLS = 256


def _attn_kernel(js_ref, q_ref, k_ref, v_ref, o_ref, sa_scr, sb_scr, m_scr, acc_scr, *, hd):
    i = pl.program_id(1)
    js = js_ref[pl.program_id(0) * pl.num_programs(1) + i]
    n_full = i - js
    q_t = q_ref[...]
    m_scr[...] = jnp.full(m_scr.shape, -jnp.inf, F32)
    acc_scr[...] = jnp.zeros(acc_scr.shape, F32)

    def scores(j, dst):
        dst[...] = jnp.dot(k_ref[j], q_t, preferred_element_type=F32)

    def consume(j, src, masked):
        tq = src.shape[1]
        for c0 in range(0, tq, ATTN_QCOLS):
            cols = slice(c0, c0 + ATTN_QCOLS)
            s_t = src[:, cols]
            if masked:
                key = lax.broadcasted_iota(jnp.int32, s_t.shape, 0)
                qry = lax.broadcasted_iota(jnp.int32, s_t.shape, 1) + c0
                s_t = jnp.where(key <= qry, s_t, -jnp.inf)
            m_old = m_scr[:, cols]
            m_new = jnp.maximum(m_old, jnp.max(s_t, axis=0, keepdims=True))
            alpha = jnp.exp2(m_old - m_new)
            p_t = jnp.exp2((s_t - m_new).astype(BF16))
            acc_scr[:, cols] = alpha * acc_scr[:, cols] + jnp.dot(v_ref[j], p_t,
                                                                  preferred_element_type=F32)
            m_scr[:, cols] = m_new

    scores(js, sa_scr)

    def pair(jj, c):
        t0 = js + 2 * jj
        scores(t0 + 1, sb_scr)
        consume(t0, sa_scr, False)
        scores(t0 + 2, sa_scr)
        consume(t0 + 1, sb_scr, False)
        return c

    lax.fori_loop(0, lax.shift_right_logical(n_full, 1), pair, 0)

    @pl.when((n_full & 1) == 0)
    def _():
        consume(i, sa_scr, True)

    @pl.when((n_full & 1) == 1)
    def _():
        scores(i, sb_scr)
        consume(i - 1, sa_scr, False)
        consume(i, sb_scr, True)

    acc = acc_scr[...]
    o_ref[...] = (acc[:hd] / acc[hd:hd + 1]).astype(o_ref.dtype)


ATTN_DEAD_LOG2 = 152.0


def _first_live_tile(q_t, kh, f_log2, tq):
    nh, _, l = q_t.shape
    nt = l // tq
    qn = jnp.sqrt(jnp.sum(jnp.square(q_t.astype(F32)), axis=1))
    kn = jnp.sqrt(jnp.sum(jnp.square(kh.astype(F32)), axis=2))
    qn_t = jnp.max(qn.reshape(nh, nt, tq), axis=2)
    kn_t = jnp.max(kn.reshape(nh, nt, tq), axis=2)
    k_max = jnp.max(kn_t, axis=1, keepdims=True)
    f = f_log2.T.reshape(nh, nt, tq)
    reach = 1.01 * qn_t * (k_max + kn_t) + f[:, :, 0]
    gap = reach[:, :, None] - f[:, None, :, tq - 1]
    below = jnp.arange(nt)[None, :] < jnp.arange(nt)[:, None]
    dead = ((gap < -ATTN_DEAD_LOG2) & below[None]).astype(jnp.int32)
    return jnp.sum(jnp.cumprod(dead, axis=2), axis=2).reshape(-1)


def _attn_prompt(q_aug, k_aug, v_aug, fcum, hd):
    nh, nt, _, tq = v_aug.shape
    l = nt * tq
    d = nh * hd
    js = _first_live_tile(q_aug[:, :hd, :], k_aug[:, :, :hd], fcum * LOG2E, tq)
    k_aug = k_aug.reshape(nh, nt, tq, V7X_LANES)
    o_t = pl.pallas_call(
        functools.partial(_attn_kernel, hd=hd),
        grid_spec=pltpu.PrefetchScalarGridSpec(
            num_scalar_prefetch=1,
            grid=(nh, nt),
            in_specs=[pl.BlockSpec((None, V7X_LANES, tq), lambda h, i, js_ref: (h, 0, i)),
                      pl.BlockSpec((None, nt, tq, V7X_LANES), lambda h, i, js_ref: (h, 0, 0, 0)),
                      pl.BlockSpec((None, nt, ATTN_VROWS, tq), lambda h, i, js_ref: (h, 0, 0, 0))],
            out_specs=pl.BlockSpec((None, hd, tq), lambda h, i, js_ref: (h, 0, i)),
            scratch_shapes=[pltpu.VMEM((tq, tq), F32), pltpu.VMEM((tq, tq), F32),
                            pltpu.VMEM((1, tq), F32), pltpu.VMEM((ATTN_VROWS, tq), F32)]),
        out_shape=jax.ShapeDtypeStruct((nh, hd, l), BF16),
        compiler_params=_cparams(("arbitrary", "arbitrary")),
        name="attn_prompt",
    )(js, q_aug, k_aug, v_aug)
    return o_t.transpose(2, 0, 1).reshape(l, d)


def _attn_paged_kernel(pt_ref, qb_ref, knew_ref, vnew_ref, lfnew_ref, *refs, pages_per_step, hd):
    del pt_ref
    pps = pages_per_step
    k_refs = refs[:pps]
    v_refs = refs[pps:2 * pps]
    lf_refs = refs[2 * pps:3 * pps]
    o_ref, m_scr, l_scr, f_scr, s_scr, p_scr, a_scr, acc_scr = refs[3 * pps:]
    nh = N_HEADS
    d, ps = acc_scr.shape
    step = pl.program_id(1)

    @pl.when(step == 0)
    def _():
        m_scr[...] = jnp.full(m_scr.shape, -jnp.inf, F32)
        l_scr[...] = jnp.zeros(l_scr.shape, F32)
        f_scr[...] = jnp.zeros(f_scr.shape, F32)
        acc_scr[...] = jnp.zeros(acc_scr.shape, F32)

    def update(k_rows, v_rows, fs, n_valid):
        n = len(fs)
        for r in range(n):
            for h in range(nh):
                rows = slice(h * hd, (h + 1) * hd)
                s_scr[h:h + 1, r * ps:(r + 1) * ps] = jnp.sum(k_rows(r, rows) * qb_ref[rows, :],
                                                             axis=0, keepdims=True)
        z = s_scr[:, :n * ps] - jnp.concatenate(fs, axis=1)
        if n_valid is not None:
            z = jnp.where(lax.broadcasted_iota(jnp.int32, z.shape, 1) % ps < n_valid, z, -jnp.inf)
        m_old = m_scr[...]
        m_new = jnp.maximum(m_old, jnp.max(z, axis=1, keepdims=True))
        alpha = jnp.exp(m_old - m_new)
        p = jnp.exp(z - m_new)
        l_scr[...] = alpha * l_scr[...] + jnp.sum(p, axis=1, keepdims=True)
        m_scr[...] = m_new
        p_scr[:, :n * ps] = p
        a_scr[...] = jnp.broadcast_to(alpha, (nh, ps))
        for h in range(nh):
            rows = slice(h * hd, (h + 1) * hd)
            acc = jnp.broadcast_to(a_scr[h:h + 1, :], (hd, ps)) * acc_scr[rows, :]
            for r in range(n):
                acc = acc + jnp.broadcast_to(p_scr[h:h + 1, r * ps:(r + 1) * ps], (hd, ps)) * v_rows(r, rows)
            acc_scr[rows, :] = acc

    tri = (lax.broadcasted_iota(jnp.int32, (ps, ps), 0)
           <= lax.broadcasted_iota(jnp.int32, (ps, ps), 1)).astype(F32)
    fs = []
    carry = f_scr[...]
    for r in range(pps):
        fs.append(jnp.dot(lf_refs[r][...], tri, precision=HIGHEST, preferred_element_type=F32) + carry)
        carry = fs[-1][:, ps - 1:ps]
    update(lambda r, rows: k_refs[r][rows, :], lambda r, rows: v_refs[r][rows, :], fs, None)
    f_scr[...] = carry

    @pl.when(step == pl.num_programs(1) - 1)
    def _():
        f_new = jnp.broadcast_to(f_scr[...] + lfnew_ref[...], (nh, ps))
        update(lambda r, rows: jnp.broadcast_to(knew_ref[rows, :], (hd, ps)),
               lambda r, rows: jnp.broadcast_to(vnew_ref[rows, :], (hd, ps)), [f_new], 1)
        for h in range(nh):
            rows = slice(h * hd, (h + 1) * hd)
            o_h = jnp.sum(acc_scr[rows, :], axis=1, keepdims=True)
            o_ref[rows, :] = o_h / l_scr[h:h + 1, :]


def _attn_paged(q_bf16, k_new, v_new, lf_new, cache_k, cache_v, cache_logf, page_table,
                pages_per_step=16):
    b, d = k_new.shape
    nh = N_HEADS
    hd = d // nh
    n_pages = page_table.shape[1]
    n_phys, ps = cache_k.shape[0], cache_k.shape[1]
    pps = math.gcd(pages_per_step, n_pages)
    ck = cache_k.transpose(0, 2, 3, 1).reshape(n_phys, d, ps)
    cv = cache_v.transpose(0, 2, 3, 1).reshape(n_phys, d, ps)
    clf = cache_logf.transpose(0, 2, 1).astype(F32)
    qb = jnp.broadcast_to(q_bf16.astype(F32)[:, :, None], (b, d, ps))
    pt = page_table.reshape(-1).astype(jnp.int32)

    def page_map(r):
        return lambda s, g, pt_ref: (pt_ref[s * n_pages + g * pps + r], 0, 0)

    seq = lambda rows, width: pl.BlockSpec((None, rows, width), lambda s, g, pt_ref: (s, 0, 0))
    in_specs = ([seq(d, ps), seq(d, 1), seq(d, 1), seq(nh, 1)]
                + [pl.BlockSpec((None, d, ps), page_map(r % pps)) for r in range(2 * pps)]
                + [pl.BlockSpec((None, nh, ps), page_map(r)) for r in range(pps)])
    out = pl.pallas_call(
        functools.partial(_attn_paged_kernel, pages_per_step=pps, hd=hd),
        grid_spec=pltpu.PrefetchScalarGridSpec(
            num_scalar_prefetch=1,
            grid=(b, n_pages // pps),
            in_specs=in_specs,
            out_specs=seq(d, 1),
            scratch_shapes=[pltpu.VMEM((nh, 1), F32), pltpu.VMEM((nh, 1), F32), pltpu.VMEM((nh, 1), F32),
                            pltpu.VMEM((nh, pps * ps), F32), pltpu.VMEM((nh, pps * ps), F32),
                            pltpu.VMEM((nh, ps), F32), pltpu.VMEM((d, ps), F32)]),
        out_shape=jax.ShapeDtypeStruct((b, d, 1), F32),
        compiler_params=_cparams(("arbitrary", "arbitrary")),
        name="attn_paged",
    )(pt, qb, k_new.reshape(b, d, 1), v_new.reshape(b, d, 1), lf_new.reshape(b, nh, 1),
      *([ck] * pps), *([cv] * pps), *([clf] * pps))
    return out.reshape(b, d).astype(BF16)


def _row_tile(m):
    return min(m, 512)


def _layer_mods(mod_all, kv_all, rows, d):
    per_layer = [[mod_all[layer, rows, i * d:(i + 1) * d] for i in range(6)]
                 for layer in range(mod_all.shape[0])]
    kv = [kv_all[0, rows, i * d:(i + 1) * d] for i in range(2)]
    return per_layer, kv


def kernel(x_prompt, x_sample, c_prompt, c_sample, state_ssm_re, state_ssm_im, cache_k, cache_v, cache_logf, page_table, ada_w, ada_b, g_mix, g_mlp, mlp_w1, mlp_w2, ssm_lam_re, ssm_lam_im, ssm_log_dt, ssm_b_re, ssm_b_im, ssm_c_re, ssm_c_im, ssm_d, w_glu, kv_ada_w, kv_ada_b, g_kv, w_kvf, b_f, k_norm, w_q, q_norm, w_o):
    bp, l, d = x_prompt.shape
    bs = x_sample.shape[0]
    nh = N_HEADS
    hd = d // nh
    assert bp == 1 and x_sample.shape[1] == 1 and ada_w.shape[0] == 2
    assert l % (S5_SEGMENTS * S5_ROWS_PER_SEG) == 0 and S5_ROWS_PER_SEG % (2 * S5_CHUNK) == 0

    row = lambda a: a.reshape(1, -1).astype(F32)
    w_glu_b = w_glu[0].astype(BF16)
    w1_b = mlp_w1.astype(BF16)
    w2_b = mlp_w2.astype(BF16)
    w_q_b = w_q[0].astype(BF16)
    w_o_b = w_o[0].astype(BF16)
    w_kvf_b = jnp.concatenate(
        [w_kvf, jnp.zeros((d, V7X_LANES - nh), w_kvf.dtype)], axis=1).astype(BF16)
    b_f_p = jnp.concatenate([b_f.astype(F32), jnp.zeros((V7X_LANES - nh,), F32)]).reshape(1, -1)
    head_of = jnp.arange(d) // hd
    e = (head_of[:, None] == jnp.arange(V7X_LANES)[None, :]).astype(F32)
    e_mean = (e / hd).astype(BF16)
    e_t = e.T.astype(BF16)
    k_norm_full = jnp.tile(k_norm.astype(F32), nh).reshape(1, d)
    q_norm_full = jnp.tile(q_norm[0].astype(F32), nh).reshape(1, d)

    ab_re, ab_im, bb_re, bb_im = _s5_discretise(ssm_lam_re[0], ssm_lam_im[0], ssm_log_dt[0],
                                                ssm_b_re[0], ssm_b_im[0])
    n_local = l // S5_CHUNK // S5_SEGMENTS
    s5_w = _s5_chunk_weights(ab_re, ab_im, bb_re, bb_im, ssm_c_re[0], ssm_c_im[0], n_local)

    n_c = bp + bs
    c_rows = -(-n_c // 8) * 8
    c_all = jnp.concatenate([c_prompt, c_sample, jnp.zeros((c_rows - n_c, d), F32)], axis=0)
    mod_all = _mods(c_all, ada_w, ada_b.reshape(ada_b.shape[0], 1, -1))
    kv_all = _mods(c_all, kv_ada_w[None], kv_ada_b.reshape(1, 1, -1))

    def trunk(x, rows, s5_fn, qkv_attend):
        m = x.shape[0]
        tm = _row_tile(m)
        (l0, l1), (sh_kv, sc_kv) = _layer_mods(mod_all, kv_all, rows, d)
        sh_m, sc_m, g_m, sh_f, sc_f, g_f = l0
        mlp0 = (row(g_mlp[0]), sh_f, sc_f, g_f, w1_b[0], w2_b[0])
        h, st_re, st_im = s5_fn(x, row(g_mix[0]), sh_m, sc_m, g_m, mlp0, tm)
        sh_m, sc_m, g_m, sh_f, sc_f, g_f = l1
        qkv_args = (h, row(g_kv), sh_kv, sc_kv, row(g_mix[1]), sh_m, sc_m,
                    w_kvf_b, b_f_p, w_q_b, k_norm_full, q_norm_full, e_mean, e_t, tm)
        k, v, lf, o = qkv_attend(qkv_args)
        h = _oproj_mlp(o, h, g_m, w_o_b, row(g_mlp[1]), sh_f, sc_f, g_f, w1_b[1], w2_b[1], tm)
        return h, st_re, st_im, k, v, lf

    def s5_prompt(x, g, sh, sc, gate, mlp0, tm):
        y, st_re, st_im = _s5_prompt(_ada_rms_chunked(x, g, sh, sc), s5_w)
        return _glu_mlp_chunked(x, y, g, row(ssm_d[0]), sh, sc, gate, w_glu_b, *mlp0), st_re, st_im

    def attend_prompt(qkv_args):
        k_t, v_t, lf_t, fcum, q_aug, k_aug, v_aug = _qkv_prompt(*qkv_args, LOG2E / math.sqrt(hd))
        o = _attn_prompt(q_aug, k_aug, v_aug, fcum, hd)
        tokens_last = lambda a: a.reshape(nh, hd, l).transpose(2, 0, 1)[None]
        return tokens_last(k_t), tokens_last(v_t), lf_t.T[None], o

    y_p, sre_p, sim_p, k_p, v_p, f_p = trunk(x_prompt[0], slice(0, 1), s5_prompt, attend_prompt)

    def s5_sample(x, g, sh, sc, gate, mlp0, tm):
        u = _ada_rms(x, g, sh, sc, tm, F32)
        y, st_re, st_im = _s5_step(u, state_ssm_re[0], state_ssm_im[0], ab_re, ab_im, bb_re, bb_im,
                                   ssm_c_re[0], ssm_c_im[0])
        h = _glu(x, y, g, row(ssm_d[0]), sh, sc, gate, w_glu_b, tm)
        return _mlp(h, *mlp0, tm), st_re, st_im

    def attend_sample(qkv_args):
        k, v, lf, q = _qkv(*qkv_args, 1.0 / math.sqrt(hd))
        o = _attn_paged(q, k, v, lf, cache_k, cache_v, cache_logf, page_table)
        return k.reshape(bs, 1, nh, hd), v.reshape(bs, 1, nh, hd), lf[:, None], o

    y_s, sre_s, sim_s, k_s, v_s, f_s = trunk(x_sample[:, 0], slice(bp, bp + bs), s5_sample,
                                             attend_sample)

    sdt = state_ssm_re.dtype
    return (y_p[None], y_s[:, None],
            sre_p[None, None].astype(sdt), sim_p[None, None].astype(sdt), k_p, v_p, f_p,
            sre_s[None].astype(sdt), sim_s[None].astype(sdt), k_s, v_s, f_s)
```
